```python
import math
import jax, jax.numpy as jnp
from jax import lax
import numpy as np

D_MODEL = 1024
BATCH = 32
SEQ = 2048
DEPTH = 2

N_META = 16
CHUNK = 128
Q_BLOCK = 128
CONV_W = 256
CONV_K = 3
RET_HEADS = 4
RET_DK = 64
RET_DV = 64
MLA_HEADS = 8
MLA_NOPE = 64
MLA_ROPE = 32
MLA_V = 64
MLA_QK = MLA_NOPE + MLA_ROPE
Q_LORA = 256
KV_LORA = 128
MIX_OUT = CONV_W + RET_HEADS * RET_DV + MLA_HEADS * MLA_V
D_FF = 4 * D_MODEL
ROPE_BASE = 10000.0
NORM_EPS = 1e-6
IN_SIZES = (CONV_W, CONV_W, CONV_W,
            RET_HEADS * RET_DK, RET_HEADS * RET_DK, RET_HEADS * RET_DV, RET_HEADS * RET_DV,
            Q_LORA, KV_LORA, MLA_ROPE)
D_IN = sum(IN_SIZES)

kernel_name = "hymba_conv_retention_mla_hybrid"


def rms_norm(x, g):
    xf = x.astype(jnp.float32)
    y = xf * lax.rsqrt(jnp.mean(jnp.square(xf), axis=-1, keepdims=True) + NORM_EPS)
    return (y * g.astype(jnp.float32)).astype(x.dtype)


def rope_tables(n_pos, dim, dtype):
    inv = 1.0 / (ROPE_BASE ** (jnp.arange(0, dim, 2, dtype=jnp.float32) / dim))
    ang = jnp.arange(n_pos, dtype=jnp.float32)[:, None] * inv[None, :]
    return jnp.cos(ang).astype(dtype), jnp.sin(ang).astype(dtype)


def apply_rope(x, cos, sin):
    x1, x2 = jnp.split(x, 2, axis=-1)
    c = cos[None, :, None, :]
    s = sin[None, :, None, :]
    return jnp.concatenate([x1 * c - x2 * s, x2 * c + x1 * s], axis=-1)


def causal_dwconv(u, w):
    return lax.conv_general_dilated(
        u, w[:, None, :].astype(u.dtype), window_strides=(1,), padding=[(CONV_K - 1, 0)],
        dimension_numbers=('NWC', 'WIO', 'NWC'), feature_group_count=u.shape[-1])


def chunkwise_retention(q, k, v):
    B, T, H, dk = q.shape
    dv = v.shape[-1]
    pad = CHUNK - N_META
    f32 = jnp.float32
    padw = ((0, 0), (pad, 0), (0, 0), (0, 0))
    qp = jnp.pad(q.astype(f32), padw)
    kp = jnp.pad(k.astype(f32), padw)
    vp = jnp.pad(v.astype(f32), padw)
    n = (T + pad) // CHUNK

    def to_chunks(a):
        return a.reshape(B, n, CHUNK, H, a.shape[-1]).transpose(1, 0, 3, 2, 4)

    log_g = jnp.log1p(-jnp.exp2(-5.0 - jnp.arange(H, dtype=f32)))
    idx = jnp.arange(CHUNK, dtype=f32)
    diff = idx[:, None] - idx[None, :]
    decay = jnp.where(diff >= 0, jnp.exp(jnp.maximum(diff, 0.0)[None] * log_g[:, None, None]), 0.0)
    xi = jnp.exp((idx[None, :] + 1.0) * log_g[:, None])[..., None]
    zeta = jnp.exp((CHUNK - 1.0 - idx[None, :]) * log_g[:, None])[..., None]
    chunk_decay = jnp.exp(CHUNK * log_g)[:, None, None]

    def step(state, xs):
        qc, kc, vc = xs
        scores = jnp.einsum('bhid,bhjd->bhij', qc, kc) * decay
        inner = jnp.einsum('bhij,bhje->bhie', scores, vc)
        cross = jnp.einsum('bhid,bhde->bhie', qc, state) * xi
        new_state = chunk_decay * state + jnp.einsum('bhjd,bhje->bhde', kc * zeta, vc)
        return new_state, inner + cross

    state0 = jnp.zeros((B, H, dk, dv), f32)
    _, out = lax.scan(step, state0, (to_chunks(qp), to_chunks(kp), to_chunks(vp)))
    out = out.transpose(1, 0, 3, 2, 4).reshape(B, n * CHUNK, H, dv)[:, pad:]
    return out.astype(q.dtype)


def block_causal_attention(q, k, v):
    T = q.shape[1]
    scale = 1.0 / math.sqrt(q.shape[-1])
    bounds = [(0, N_META)] + [(s, min(s + Q_BLOCK, T)) for s in range(N_META, T, Q_BLOCK)]
    outs = []
    for s, e in bounds:
        qb, kb, vb = q[:, s:e], k[:, :e], v[:, :e]
        logits = jnp.einsum('bqhd,bkhd->bhqk', qb, kb).astype(jnp.float32) * scale
        mask = jnp.arange(e)[None, :] <= jnp.arange(s, e)[:, None]
        p = jax.nn.softmax(jnp.where(mask, logits, -jnp.inf), axis=-1).astype(vb.dtype)
        outs.append(jnp.einsum('bhqk,bkhd->bqhd', p, vb))
    return jnp.concatenate(outs, axis=1)


def hybrid_layer(x, cos_r, sin_r, cos_m, sin_m, attn_norm_g, w_in, conv_w, ret_gn_g,
                 q_norm_g, w_uq, kv_norm_g, w_ukv, q_head_norm_g, k_head_norm_g, w_out,
                 mlp_norm_g, w_mlp_in, w_mlp_out):
    B, T, _ = x.shape
    h = rms_norm(x, attn_norm_g)
    proj = h @ w_in
    offs = []
    acc = 0
    for sz in IN_SIZES[:-1]:
        acc += sz
        offs.append(acc)
    cb, cc, ch, rq, rk, rv, rg, cq, ckv, kr = jnp.split(proj, offs, axis=-1)

    y_conv = cb * causal_dwconv(cc * ch, conv_w)

    q_r = apply_rope(rq.reshape(B, T, RET_HEADS, RET_DK), cos_r, sin_r)
    k_r = apply_rope(rk.reshape(B, T, RET_HEADS, RET_DK), cos_r, sin_r) * (RET_DK ** -0.5)
    v_r = rv.reshape(B, T, RET_HEADS, RET_DV)
    o_r = rms_norm(chunkwise_retention(q_r, k_r, v_r), ret_gn_g)
    y_ret = jax.nn.silu(rg) * o_r.reshape(B, T, RET_HEADS * RET_DV)

    q_m = (rms_norm(cq, q_norm_g) @ w_uq).reshape(B, T, MLA_HEADS, MLA_QK)
    kv = (rms_norm(ckv, kv_norm_g) @ w_ukv).reshape(B, T, MLA_HEADS, MLA_NOPE + MLA_V)
    k_nope, v_m = kv[..., :MLA_NOPE], kv[..., MLA_NOPE:]
    k_rope = jnp.broadcast_to(kr[:, :, None, :], (B, T, MLA_HEADS, MLA_ROPE))
    k_m = jnp.concatenate([k_nope, k_rope], axis=-1)
    q_m = rms_norm(q_m, q_head_norm_g)
    k_m = rms_norm(k_m, k_head_norm_g)
    q_m = jnp.concatenate([q_m[..., :MLA_NOPE], apply_rope(q_m[..., MLA_NOPE:], cos_m, sin_m)], axis=-1)
    k_m = jnp.concatenate([k_m[..., :MLA_NOPE], apply_rope(k_m[..., MLA_NOPE:], cos_m, sin_m)], axis=-1)
    y_mla = block_causal_attention(q_m, k_m, v_m).reshape(B, T, MLA_HEADS * MLA_V)

    x = x + jnp.concatenate([y_conv, y_ret, y_mla], axis=-1) @ w_out

    u = rms_norm(x, mlp_norm_g) @ w_mlp_in
    return x + jnp.square(jax.nn.relu(u)) @ w_mlp_out


def setup_inputs(seed: int = 0) -> dict:
    key = jax.random.key(seed)
    ks = jax.random.split(key, 16)
    nrm = jax.random.normal

    def gain(k, shape):
        return 1.0 + 0.02 * nrm(k, shape, jnp.float32)

    return {
        "x": nrm(ks[0], (BATCH, SEQ, D_MODEL), jnp.float32),
        "meta_tokens": nrm(ks[1], (N_META, D_MODEL), jnp.float32),
        "attn_norm_g": gain(ks[2], (DEPTH, D_MODEL)),
        "w_in": nrm(ks[3], (DEPTH, D_MODEL, D_IN), jnp.float32) * D_MODEL ** -0.5,
        "conv_w": nrm(ks[4], (DEPTH, CONV_K, CONV_W), jnp.float32) * CONV_K ** -0.5,
        "ret_gn_g": gain(ks[5], (DEPTH, RET_HEADS, RET_DV)),
        "q_norm_g": gain(ks[6], (DEPTH, Q_LORA)),
        "w_uq": nrm(ks[7], (DEPTH, Q_LORA, MLA_HEADS * MLA_QK), jnp.float32) * Q_LORA ** -0.5,
        "kv_norm_g": gain(ks[8], (DEPTH, KV_LORA)),
        "w_ukv": nrm(ks[9], (DEPTH, KV_LORA, MLA_HEADS * (MLA_NOPE + MLA_V)), jnp.float32) * KV_LORA ** -0.5,
        "q_head_norm_g": gain(ks[10], (DEPTH, MLA_QK)),
        "k_head_norm_g": gain(ks[11], (DEPTH, MLA_QK)),
        "w_out": nrm(ks[12], (DEPTH, MIX_OUT, D_MODEL), jnp.float32) * MIX_OUT ** -0.5,
        "mlp_norm_g": gain(ks[13], (DEPTH, D_MODEL)),
        "w_mlp_in": nrm(ks[14], (DEPTH, D_MODEL, D_FF), jnp.float32) * D_MODEL ** -0.5,
        "w_mlp_out": nrm(ks[15], (DEPTH, D_FF, D_MODEL), jnp.float32) * D_FF ** -0.5,
    }


def reference(x, meta_tokens, attn_norm_g, w_in, conv_w, ret_gn_g, q_norm_g, w_uq, kv_norm_g,
              w_ukv, q_head_norm_g, k_head_norm_g, w_out, mlp_norm_g, w_mlp_in, w_mlp_out):
    B = x.shape[0]
    meta = jnp.broadcast_to(meta_tokens.astype(x.dtype)[None], (B, N_META, D_MODEL))
    h = jnp.concatenate([meta, x], axis=1)
    T = h.shape[1]
    cos_r, sin_r = rope_tables(T, RET_DK, h.dtype)
    cos_m, sin_m = rope_tables(T, MLA_ROPE, h.dtype)
    for l in range(DEPTH):
        h = hybrid_layer(h, cos_r, sin_r, cos_m, sin_m, attn_norm_g[l], w_in[l], conv_w[l],
                         ret_gn_g[l], q_norm_g[l], w_uq[l], kv_norm_g[l], w_ukv[l],
                         q_head_norm_g[l], k_head_norm_g[l], w_out[l], mlp_norm_g[l],
                         w_mlp_in[l], w_mlp_out[l])
    return h[:, N_META:]
```

```python
import functools
import math

import jax
import jax.numpy as jnp
from jax import lax
from jax.experimental import pallas as pl
from jax.experimental.pallas import tpu as pltpu

D_MODEL = 1024
SEQ = 2048
N_META = 16
CONV_W = 256
CONV_K = 3
RET_HEADS = 4
RET_DK = 64
RET_W = RET_HEADS * RET_DK
MLA_HEADS = 8
MLA_NOPE = 64
MLA_ROPE = 32
MLA_V = 64
MLA_QK = MLA_NOPE + MLA_ROPE
Q_LORA = 256
KV_LORA = 128
D_FF = 4 * D_MODEL
ROPE_BASE = 10000.0
NORM_EPS = 1e-6

LANES = 128
HEAD_W = LANES
ATT_W = MLA_HEADS * HEAD_W
MIX_HALF = CONV_W + RET_W
O_W = MLA_HEADS * MLA_V

TQ = 256
PAD = TQ - N_META
TP = SEQ + TQ
NT = TP // TQ
N_REAL_TILES = SEQ // TQ

OFF_CB, OFF_CC, OFF_CH = 0, 256, 512
OFF_RQ, OFF_RK, OFF_RV, OFF_RG = 768, 1024, 1280, 1536
OFF_CQ, OFF_CKV, OFF_KR = 1792, 2048, 2176
D_IN_EXT = OFF_KR + HEAD_W

FF_CHUNK = 1024
MASK_VALUE = -1e30
EXP2_SCALE = math.log2(math.e) / math.sqrt(MLA_QK)
VMEM_LIMIT = 56 * 1024 * 1024

_F32 = jnp.float32
_BF16 = jnp.bfloat16
_NT_DIMS = (((1,), (1,)), ((), ()))
_TN_DIMS = (((0,), (0,)), ((), ()))


def _rms(x, n):
    return x * lax.rsqrt(jnp.sum(x * x, axis=-1, keepdims=True) * (1.0 / n) + NORM_EPS)


def _rot_pairs(z, half, first):
    w = z.shape[-1]
    return jnp.where(first, pltpu.roll(z, w - half, axis=1), pltpu.roll(z, half, axis=1))


def _load_tile(first_layer, i, x_ref, meta_ref):
    x = x_ref[0]
    if first_layer:
        return jnp.where(i == 0, meta_ref[...], x)
    row = lax.broadcasted_iota(jnp.int32, (TQ, 1), 0)
    return jnp.where((i > 0) | (row >= PAD), x, 0.0)


def _pre_kernel(first_layer, *refs):
    if first_layer:
        x_ref, meta_ref = refs[:2]
        refs = refs[2:]
    else:
        x_ref, meta_ref = refs[0], None
        refs = refs[1:]
    (g_attn, w_in, conv_w, cosr, sinr, dmat, xi, zeta, cdec, bdmask, gmat, ret_g,
     qn_g, w_uq, kvn_g, w_k, w_v, vones, gq, gk, cosm, sinm,
     ycr_ref, q_ref, k_ref, v_ref, ubuf, state) = refs
    i = pl.program_id(1)

    @pl.when(i == 0)
    def _():
        ubuf[0:8, :] = jnp.zeros((8, CONV_W), _F32)
        state[...] = jnp.zeros_like(state)

    x = _load_tile(first_layer, i, x_ref, meta_ref)
    hn = (_rms(x, D_MODEL) * g_attn[...]).astype(_BF16)
    proj = jnp.dot(hn, w_in[...], preferred_element_type=_F32)

    cb = proj[:, OFF_CB:OFF_CB + CONV_W]
    u = proj[:, OFF_CC:OFF_CC + CONV_W] * proj[:, OFF_CH:OFF_CH + CONV_W]
    ubuf[8:8 + TQ, :] = u
    cw = conv_w[...]
    conv = cw[0:1] * ubuf[6:6 + TQ, :] + cw[1:2] * ubuf[7:7 + TQ, :] + cw[2:3] * u
    ubuf[0:8, :] = u[TQ - 8:, :]
    ycr_ref[0, :, 0:CONV_W] = (cb * conv).astype(_BF16)

    lane = lax.broadcasted_iota(jnp.int32, (1, RET_W), 1)
    first = (lane % RET_DK) < (RET_DK // 2)
    rq = proj[:, OFF_RQ:OFF_RQ + RET_W]
    rk = proj[:, OFF_RK:OFF_RK + RET_W]
    rv = proj[:, OFF_RV:OFF_RV + RET_W]
    rg = proj[:, OFF_RG:OFF_RG + RET_W]
    cr, sr = cosr[...], sinr[...]
    qr = rq * cr + _rot_pairs(rq, RET_DK // 2, first) * sr
    kr = (rk * cr + _rot_pairs(rk, RET_DK // 2, first) * sr) * (RET_DK ** -0.5)
    kb = kr.astype(_BF16)
    inner = jnp.zeros((TQ, RET_W), _F32)
    for h in range(RET_HEADS):
        hmask = (lane // RET_DK) == h
        qh = jnp.where(hmask, qr, 0.0).astype(_BF16)
        sc = lax.dot_general(qh, kb, _NT_DIMS, preferred_element_type=_F32)
        sc = (sc * dmat[h]).astype(_BF16)
        vh = jnp.where(hmask, rv, 0.0).astype(_BF16)
        inner = inner + jnp.dot(sc, vh, preferred_element_type=_F32)
    st = state[...]
    cross = jnp.dot(qr.astype(_BF16), st.astype(_BF16), preferred_element_type=_F32) * xi[...]
    upd = lax.dot_general((kr * zeta[...]).astype(_BF16), rv.astype(_BF16), _TN_DIMS,
                          preferred_element_type=_F32)
    state[...] = cdec[...] * st + bdmask[...] * upd
    o = inner + cross
    o2 = o * o
    o2_hi = o2.astype(_BF16)
    o2_lo = (o2 - o2_hi.astype(_F32)).astype(_BF16)
    gm = gmat[...]
    msq = (jnp.dot(o2_hi, gm, preferred_element_type=_F32)
           + jnp.dot(o2_lo, gm, preferred_element_type=_F32))
    o_n = o * lax.rsqrt(msq + NORM_EPS) * ret_g[...]
    gate = rg * (1.0 / (1.0 + jnp.exp(-rg)))
    ycr_ref[0, :, CONV_W:MIX_HALF] = (gate * o_n).astype(_BF16)

    cq = proj[:, OFF_CQ:OFF_CQ + Q_LORA]
    ckv = proj[:, OFF_CKV:OFF_CKV + KV_LORA]
    kr128 = proj[:, OFF_KR:OFF_KR + HEAD_W]
    cqn = (_rms(cq, Q_LORA) * qn_g[...]).astype(_BF16)
    ckvn = (_rms(ckv, KV_LORA) * kvn_g[...]).astype(_BF16)
    qraw = jnp.dot(cqn, w_uq[...], preferred_element_type=_F32)
    kraw = jnp.dot(ckvn, w_k[...], preferred_element_type=_F32)
    v_ref[0] = (jnp.dot(ckvn, w_v[...], preferred_element_type=_F32) + vones[...]).astype(_BF16)
    lane_h = lax.broadcasted_iota(jnp.int32, (1, HEAD_W), 1)
    first_m = lane_h < (MLA_NOPE + MLA_ROPE // 2)
    cm, sm = cosm[...], sinm[...]
    gqv, gkv = gq[...], gk[...]
    for h in range(MLA_HEADS):
        sl = slice(h * HEAD_W, (h + 1) * HEAD_W)
        qh = _rms(qraw[:, sl], MLA_QK) * gqv
        q_ref[0, :, sl] = (qh * cm + _rot_pairs(qh, MLA_ROPE // 2, first_m) * sm).astype(_BF16)
        kh = _rms(kraw[:, sl] + kr128, MLA_QK) * gkv
        k_ref[0, :, sl] = (kh * cm + _rot_pairs(kh, MLA_ROPE // 2, first_m) * sm).astype(_BF16)


def _attn_kernel(q_ref, k_ref, v_ref, o_ref, acc_scr, m_scr):
    i = pl.program_id(1)
    m_scr[...] = jnp.full(m_scr.shape, MASK_VALUE, _F32)
    acc_scr[...] = jnp.zeros(acc_scr.shape, _F32)
    rowg = i * TQ + lax.broadcasted_iota(jnp.int32, (TQ, 1), 0)

    def body(j, carry):
        start = pl.multiple_of(j * TQ, TQ)
        colg = j * TQ + lax.broadcasted_iota(jnp.int32, (1, TQ), 1)
        mask = (colg <= rowg) & (colg >= PAD)
        for h in range(MLA_HEADS):
            sl = slice(h * HEAD_W, (h + 1) * HEAD_W)
            kh = k_ref[0, pl.ds(start, TQ), sl]
            vh = v_ref[0, pl.ds(start, TQ), sl]
            s = lax.dot_general(q_ref[0, :, sl], kh, _NT_DIMS, preferred_element_type=_F32)
            s = jnp.where(mask, s, MASK_VALUE)
            m_prev = m_scr[h]
            m_new = jnp.maximum(m_prev, jnp.max(s, axis=1, keepdims=True))
            p = jnp.exp2((s - pltpu.repeat(m_new, TQ // LANES, axis=1)) * EXP2_SCALE)
            alpha = jnp.exp2((m_prev - m_new) * EXP2_SCALE)
            acc_scr[h] = acc_scr[h] * alpha + jnp.dot(p.astype(_BF16), vh, preferred_element_type=_F32)
            m_scr[h] = m_new
        return carry

    lax.fori_loop(0, i + 1, body, 0)

    lane = lax.broadcasted_iota(jnp.int32, (1, HEAD_W), 1)
    for pr in range(MLA_HEADS // 2):
        a0 = acc_scr[2 * pr]
        a1 = acc_scr[2 * pr + 1]
        o0 = a0 * pltpu.roll(1.0 / a0, MLA_V, axis=1)
        o1 = a1 * pltpu.roll(1.0 / a1, MLA_V, axis=1)
        o_ref[0, :, pr * HEAD_W:(pr + 1) * HEAD_W] = jnp.where(lane < MLA_V, o0, o1).astype(_BF16)


def _post_kernel(first_layer, last_layer, *refs):
    if first_layer:
        x_ref, meta_ref = refs[:2]
        refs = refs[2:]
    else:
        x_ref, meta_ref = refs[0], None
        refs = refs[1:]
    ycr_ref, o_ref, w_out, g_mlp, w1, w2, out_ref = refs
    if last_layer:
        x = x_ref[0]
    else:
        x = _load_tile(first_layer, pl.program_id(1), x_ref, meta_ref)
    x1 = (x + jnp.dot(ycr_ref[0], w_out[0:MIX_HALF, :], preferred_element_type=_F32)
          + jnp.dot(o_ref[0], w_out[MIX_HALF:, :], preferred_element_type=_F32))
    hn = (_rms(x1, D_MODEL) * g_mlp[...]).astype(_BF16)
    up = jnp.dot(hn, w1[...], preferred_element_type=_F32)
    act = jnp.square(jnp.maximum(up, 0.0)).astype(_BF16)
    out_ref[0] = x1 + jnp.dot(act, w2[...], preferred_element_type=_F32)


def _const_spec(shape):
    nd = len(shape)
    return pl.BlockSpec(shape, lambda b, i, _nd=nd: (0,) * _nd, pipeline_mode=pl.Buffered(1))


def _params():
    return pltpu.CompilerParams(dimension_semantics=("parallel", "arbitrary"),
                                vmem_limit_bytes=VMEM_LIMIT)


def _tables():
    pos = jnp.maximum(jnp.arange(TP, dtype=_F32) - PAD, 0.0)
    inv_r = 1.0 / (ROPE_BASE ** (jnp.arange(0, RET_DK, 2, dtype=_F32) / RET_DK))
    ang_r = pos[:, None] * inv_r[None, :]
    l = jnp.arange(RET_W)
    cosr = jnp.cos(ang_r)[:, l % (RET_DK // 2)]
    sinr = jnp.sin(ang_r)[:, l % (RET_DK // 2)] * jnp.where((l % RET_DK) < RET_DK // 2, -1.0, 1.0)[None, :]
    inv_m = 1.0 / (ROPE_BASE ** (jnp.arange(0, MLA_ROPE, 2, dtype=_F32) / MLA_ROPE))
    ang_m = pos[:, None] * inv_m[None, :]
    lh = jnp.arange(HEAD_W)
    in_rope = (lh >= MLA_NOPE) & (lh < MLA_QK)
    fidx = jnp.clip(lh - MLA_NOPE, 0, MLA_ROPE - 1) % (MLA_ROPE // 2)
    cosm = jnp.where(in_rope[None, :], jnp.cos(ang_m)[:, fidx], 1.0)
    sgn = jnp.where(lh < MLA_NOPE + MLA_ROPE // 2, -1.0, 1.0)
    sinm = jnp.where(in_rope[None, :], jnp.sin(ang_m)[:, fidx] * sgn[None, :], 0.0)
    log_g = jnp.log1p(-jnp.exp2(-5.0 - jnp.arange(RET_HEADS, dtype=_F32)))
    idx = jnp.arange(TQ, dtype=_F32)
    diff = idx[:, None] - idx[None, :]
    dmat = jnp.where(diff >= 0, jnp.exp(jnp.maximum(diff, 0.0)[None] * log_g[:, None, None]), 0.0)
    lg_lane = log_g[l // RET_DK]
    xi = jnp.exp((idx[:, None] + 1.0) * lg_lane[None, :])
    zeta = jnp.exp((TQ - 1.0 - idx[:, None]) * lg_lane[None, :])
    cdec = jnp.exp(TQ * lg_lane)[None, :]
    same = (l[:, None] // RET_DK) == (l[None, :] // RET_DK)
    bdmask = same.astype(_F32)
    gmat = (same.astype(_F32) / RET_DK).astype(_BF16)
    return dict(cosr=cosr, sinr=sinr, cosm=cosm, sinm=sinm, dmat=dmat, xi=xi, zeta=zeta, cdec=cdec,
                bdmask=bdmask, gmat=gmat)


def _layer_weights(l, attn_norm_g, w_in, conv_w, ret_gn_g, q_norm_g, w_uq, kv_norm_g, w_ukv,
                   q_head_norm_g, k_head_norm_g, w_out, mlp_norm_g, w_mlp_in, w_mlp_out):
    wi = w_in[l]
    w_in_ext = jnp.concatenate(
        [wi[:, :OFF_KR], jnp.zeros((D_MODEL, MLA_NOPE), _F32), wi[:, OFF_KR:],
         jnp.zeros((D_MODEL, HEAD_W - MLA_QK), _F32)], axis=1).astype(_BF16)
    wq = w_uq[l].reshape(Q_LORA, MLA_HEADS, MLA_QK)
    w_uq_p = jnp.pad(wq, ((0, 0), (0, 0), (0, HEAD_W - MLA_QK))).reshape(Q_LORA, ATT_W).astype(_BF16)
    wkv = w_ukv[l].reshape(KV_LORA, MLA_HEADS, MLA_NOPE + MLA_V)
    w_k = jnp.pad(wkv[:, :, :MLA_NOPE], ((0, 0), (0, 0), (0, HEAD_W - MLA_NOPE))).reshape(KV_LORA, ATT_W)
    wv = wkv[:, :, MLA_NOPE:].reshape(KV_LORA, MLA_HEADS // 2, 2, MLA_V)
    zv = jnp.zeros((KV_LORA, MLA_HEADS // 2, MLA_V), _F32)
    w_v = jnp.stack([wv[:, :, 0], zv, zv, wv[:, :, 1]], axis=2).reshape(KV_LORA, ATT_W)
    ones = jnp.ones((MLA_HEADS // 2, MLA_V), _F32)
    vones = jnp.stack([0 * ones, ones, ones, 0 * ones], axis=1).reshape(1, ATT_W)
    pad_g = lambda g: jnp.pad(g, (0, HEAD_W - MLA_QK))[None, :]
    return dict(
        g_attn=attn_norm_g[l][None, :], w_in=w_in_ext, conv_w=conv_w[l], ret_g=ret_gn_g[l].reshape(1, RET_W),
        qn_g=q_norm_g[l][None, :], w_uq=w_uq_p, kvn_g=kv_norm_g[l][None, :], w_k=w_k.astype(_BF16),
        w_v=w_v.astype(_BF16), vones=vones, gq=pad_g(q_head_norm_g[l]), gk=pad_g(k_head_norm_g[l]),
        w_out=w_out[l].astype(_BF16), g_mlp=mlp_norm_g[l][None, :], w1=w_mlp_in[l].astype(_BF16),
        w2=w_mlp_out[l].astype(_BF16))


def _x_specs(first_layer):
    if first_layer:
        return [pl.BlockSpec((1, TQ, D_MODEL), lambda b, i: (b, jnp.maximum(i - 1, 0), 0)),
                _const_spec((TQ, D_MODEL))]
    return [pl.BlockSpec((1, TQ, D_MODEL), lambda b, i: (b, i, 0))]


def _pre_call(first_layer, batch, x_args, w, t):
    tile = lambda width: pl.BlockSpec((1, TQ, width), lambda b, i: (b, i, 0))
    row_tab = lambda width: pl.BlockSpec((TQ, width), lambda b, i: (i, 0))
    const_names = ["g_attn", "w_in", "conv_w"]
    in_specs = _x_specs(first_layer) + [_const_spec(w[n].shape) for n in const_names]
    args = list(x_args) + [w[n] for n in const_names]
    in_specs += [row_tab(RET_W), row_tab(RET_W)]
    args += [t["cosr"], t["sinr"]]
    for n, src in [("dmat", t), ("xi", t), ("zeta", t), ("cdec", t), ("bdmask", t), ("gmat", t), ("ret_g", w),
                   ("qn_g", w), ("w_uq", w), ("kvn_g", w), ("w_k", w), ("w_v", w), ("vones", w), ("gq", w),
                   ("gk", w)]:
        in_specs.append(_const_spec(src[n].shape))
        args.append(src[n])
    in_specs += [row_tab(HEAD_W), row_tab(HEAD_W)]
    args += [t["cosm"], t["sinm"]]
    out_shape = [jax.ShapeDtypeStruct((batch, TP, MIX_HALF), _BF16),
                 jax.ShapeDtypeStruct((batch, TP, ATT_W), _BF16),
                 jax.ShapeDtypeStruct((batch, TP, ATT_W), _BF16),
                 jax.ShapeDtypeStruct((batch, TP, ATT_W), _BF16)]
    return pl.pallas_call(
        functools.partial(_pre_kernel, first_layer),
        grid=(batch, NT), in_specs=in_specs,
        out_specs=[tile(MIX_HALF), tile(ATT_W), tile(ATT_W), tile(ATT_W)],
        out_shape=out_shape,
        scratch_shapes=[pltpu.VMEM((TQ + 8, CONV_W), _F32), pltpu.VMEM((RET_W, RET_W), _F32)],
        compiler_params=_params(), name="pre")(*args)


def _attn_call(batch, q, k, v):
    seq = pl.BlockSpec((1, TP, ATT_W), lambda b, i: (b, 0, 0))
    return pl.pallas_call(
        _attn_kernel, grid=(batch, NT),
        in_specs=[pl.BlockSpec((1, TQ, ATT_W), lambda b, i: (b, i, 0)), seq, seq],
        out_specs=pl.BlockSpec((1, TQ, O_W), lambda b, i: (b, i, 0)),
        out_shape=jax.ShapeDtypeStruct((batch, TP, O_W), _BF16),
        scratch_shapes=[pltpu.VMEM((MLA_HEADS, TQ, HEAD_W), _F32), pltpu.VMEM((MLA_HEADS, TQ, HEAD_W), _F32)],
        compiler_params=_params(), name="attn")(q, k, v)


def _post_call(first_layer, last_layer, batch, x_args, ycr, o, w):
    off = 1 if last_layer else 0
    tile = lambda width: pl.BlockSpec((1, TQ, width), lambda b, i: (b, i + off, 0))
    if last_layer:
        x_specs = [pl.BlockSpec((1, TQ, D_MODEL), lambda b, i: (b, i + off, 0))]
        out_shape = jax.ShapeDtypeStruct((batch, SEQ, D_MODEL), _F32)
        n_tiles = N_REAL_TILES
    else:
        x_specs = _x_specs(first_layer)
        out_shape = jax.ShapeDtypeStruct((batch, TP, D_MODEL), _F32)
        n_tiles = NT
    const_names = ["w_out", "g_mlp", "w1", "w2"]
    return pl.pallas_call(
        functools.partial(_post_kernel, first_layer, last_layer),
        grid=(batch, n_tiles),
        in_specs=x_specs + [tile(MIX_HALF), tile(O_W)] + [_const_spec(w[n].shape) for n in const_names],
        out_specs=pl.BlockSpec((1, TQ, D_MODEL), lambda b, i: (b, i, 0)),
        out_shape=out_shape, compiler_params=_params(), name="post")(
            *x_args, ycr, o, *[w[n] for n in const_names])


def kernel(x, meta_tokens, attn_norm_g, w_in, conv_w, ret_gn_g, q_norm_g, w_uq, kv_norm_g, w_ukv,
           q_head_norm_g, k_head_norm_g, w_out, mlp_norm_g, w_mlp_in, w_mlp_out):
    batch = x.shape[0]
    depth = w_in.shape[0]
    assert x.shape[1:] == (SEQ, D_MODEL) and depth >= 2
    t = _tables()
    meta_pad = jnp.concatenate([jnp.zeros((PAD, D_MODEL), x.dtype), meta_tokens.astype(x.dtype)], axis=0)
    h = None
    for l in range(depth):
        w = _layer_weights(l, attn_norm_g, w_in, conv_w, ret_gn_g, q_norm_g, w_uq, kv_norm_g, w_ukv,
                           q_head_norm_g, k_head_norm_g, w_out, mlp_norm_g, w_mlp_in, w_mlp_out)
        first, last = l == 0, l == depth - 1
        x_args = (x, meta_pad) if first else (h,)
        ycr, q, k, v = _pre_call(first, batch, x_args, w, t)
        o = _attn_call(batch, q, k, v)
        h = _post_call(first, last, batch, x_args, ycr, o, w)
    return h
```

```python
import functools
import math

import jax
import jax.numpy as jnp
from jax import lax
from jax.experimental import pallas as pl
from jax.experimental.pallas import tpu as pltpu

D_MODEL = 1024
SEQ = 2048
N_META = 16
CONV_W = 256
CONV_K = 3
RET_HEADS = 4
RET_DK = 64
RET_W = RET_HEADS * RET_DK
MLA_HEADS = 8
MLA_NOPE = 64
MLA_ROPE = 32
MLA_V = 64
MLA_QK = MLA_NOPE + MLA_ROPE
Q_LORA = 256
KV_LORA = 128
D_FF = 4 * D_MODEL
ROPE_BASE = 10000.0
NORM_EPS = 1e-6

LANES = 128
HEAD_W = LANES
ATT_W = MLA_HEADS * HEAD_W
MIX_HALF = CONV_W + RET_W
O_W = MLA_HEADS * MLA_V

TQ = 256
PAD = TQ - N_META
TP = SEQ + TQ
NT = TP // TQ
N_REAL_TILES = SEQ // TQ

OFF_CB, OFF_CC, OFF_CH = 0, 256, 512
OFF_RQ, OFF_RK, OFF_RV, OFF_RG = 768, 1024, 1280, 1536
OFF_CQ, OFF_CKV, OFF_KR = 1792, 2048, 2176
D_IN_EXT = OFF_KR + HEAD_W

MASK_VALUE = -1e30
EXP2_SCALE = math.log2(math.e) / math.sqrt(MLA_QK)
VMEM_LIMIT = 56 * 1024 * 1024

_F32 = jnp.float32
_BF16 = jnp.bfloat16
_NT_DIMS = (((1,), (1,)), ((), ()))
_TN_DIMS = (((0,), (0,)), ((), ()))


def _rms(x, n):
    return x * lax.rsqrt(jnp.sum(x * x, axis=-1, keepdims=True) * (1.0 / n) + NORM_EPS)


def _rot_pairs(z, half, first):
    w = z.shape[-1]
    return jnp.where(first, pltpu.roll(z, w - half, axis=1), pltpu.roll(z, half, axis=1))


def _load_tile(first_layer, i, x_ref, meta_ref):
    x = x_ref[0]
    if first_layer:
        return jnp.where(i == 0, meta_ref[...], x)
    row = lax.broadcasted_iota(jnp.int32, (TQ, 1), 0)
    return jnp.where((i > 0) | (row >= PAD), x, 0.0)


def _pre_kernel(first_layer, *refs):
    if first_layer:
        x_ref, meta_ref = refs[:2]
        refs = refs[2:]
    else:
        x_ref, meta_ref = refs[0], None
        refs = refs[1:]
    (g_attn, w_in, conv_w, cosr, sinr, dmat, xi, zeta, cdec, bdmask, gmat, ret_g,
     qn_g, w_uq, w_uq_rot, kvn_g, w_k, w_vt, gpair, gq, gq_rot, gk, cosm, sinm,
     ycr_ref, q_ref, k_ref, vt_ref, ubuf, state) = refs
    i = pl.program_id(1)

    @pl.when(i == 0)
    def _():
        ubuf[0:8, :] = jnp.zeros((8, CONV_W), _F32)
        state[...] = jnp.zeros_like(state)

    x = _load_tile(first_layer, i, x_ref, meta_ref)
    hn = (_rms(x, D_MODEL) * g_attn[...]).astype(_BF16)
    proj = jnp.dot(hn, w_in[...], preferred_element_type=_F32)

    cb = proj[:, OFF_CB:OFF_CB + CONV_W]
    u = proj[:, OFF_CC:OFF_CC + CONV_W] * proj[:, OFF_CH:OFF_CH + CONV_W]
    ubuf[8:8 + TQ, :] = u
    cw = conv_w[...]
    conv = cw[0:1] * ubuf[6:6 + TQ, :] + cw[1:2] * ubuf[7:7 + TQ, :] + cw[2:3] * u
    ubuf[0:8, :] = u[TQ - 8:, :]
    ycr_ref[0, :, 0:CONV_W] = (cb * conv).astype(_BF16)

    lane = lax.broadcasted_iota(jnp.int32, (1, RET_W), 1)
    first = (lane % RET_DK) < (RET_DK // 2)
    rq = proj[:, OFF_RQ:OFF_RQ + RET_W]
    rk = proj[:, OFF_RK:OFF_RK + RET_W]
    rv = proj[:, OFF_RV:OFF_RV + RET_W]
    rg = proj[:, OFF_RG:OFF_RG + RET_W]
    cr, sr = cosr[...], sinr[...]
    qr = rq * cr + _rot_pairs(rq, RET_DK // 2, first) * sr
    kr = (rk * cr + _rot_pairs(rk, RET_DK // 2, first) * sr) * (RET_DK ** -0.5)
    kb = kr.astype(_BF16)
    inner = jnp.zeros((TQ, RET_W), _F32)
    for h in range(RET_HEADS):
        hmask = (lane // RET_DK) == h
        qh = jnp.where(hmask, qr, 0.0).astype(_BF16)
        sc = lax.dot_general(qh, kb, _NT_DIMS, preferred_element_type=_F32)
        sc = (sc * dmat[h]).astype(_BF16)
        vh = jnp.where(hmask, rv, 0.0).astype(_BF16)
        inner = inner + jnp.dot(sc, vh, preferred_element_type=_F32)
    st = state[...]
    cross = jnp.dot(qr.astype(_BF16), st.astype(_BF16), preferred_element_type=_F32) * xi[...]
    upd = lax.dot_general((kr * zeta[...]).astype(_BF16), rv.astype(_BF16), _TN_DIMS,
                          preferred_element_type=_F32)
    state[...] = cdec[...] * st + bdmask[...] * upd
    o = inner + cross
    msq = jnp.dot((o * o).astype(_BF16), gmat[...], preferred_element_type=_F32)
    o_n = o * lax.rsqrt(msq + NORM_EPS) * ret_g[...]
    gate = rg * (1.0 / (1.0 + jnp.exp(-rg)))
    ycr_ref[0, :, CONV_W:MIX_HALF] = (gate * o_n).astype(_BF16)

    cq = proj[:, OFF_CQ:OFF_CQ + Q_LORA]
    ckv = proj[:, OFF_CKV:OFF_CKV + KV_LORA]
    kr128 = proj[:, OFF_KR:OFF_KR + HEAD_W]
    cqn = (_rms(cq, Q_LORA) * qn_g[...]).astype(_BF16)
    ckvn = (_rms(ckv, KV_LORA) * kvn_g[...]).astype(_BF16)
    qraw = jnp.dot(cqn, w_uq[...], preferred_element_type=_F32)
    qrot = jnp.dot(cqn, w_uq_rot[...], preferred_element_type=_F32)
    kraw = jnp.dot(ckvn, w_k[...], preferred_element_type=_F32)
    vt = lax.dot_general(w_vt[...], ckvn, _NT_DIMS, preferred_element_type=_F32)
    vrow = lax.broadcasted_iota(jnp.int32, (ATT_W, 1), 0)
    vt_ref[0, 0] = jnp.where((vrow % HEAD_W) >= MLA_V, 1.0, vt).astype(_BF16)

    lane_h = lax.broadcasted_iota(jnp.int32, (1, HEAD_W), 1)
    first_m = lane_h < (MLA_NOPE + MLA_ROPE // 2)
    cm, sm = cosm[...], sinm[...]
    gkv = gk[...]
    q_cos, q_sin = gq[...] * cm, gq_rot[...] * sm
    krg = kr128 * gkv
    k_rope = krg * cm + _rot_pairs(krg, MLA_ROPE // 2, first_m) * sm
    one_q = jnp.where(lane_h == MLA_QK, 1.0, 0.0)
    rowg = i * TQ + lax.broadcasted_iota(jnp.int32, (TQ, 1), 0)
    kbias = jnp.where((rowg < PAD) & (lane_h == MLA_QK), MASK_VALUE, 0.0)
    g2 = gpair[...]
    q2 = (qraw * qraw).astype(_BF16)
    k2 = (kraw * kraw).astype(_BF16)
    kr_ms = jnp.dot((kr128 * kr128).astype(_BF16), g2[0:HEAD_W, 0:HEAD_W], preferred_element_type=_F32)
    kr_ms2 = jnp.concatenate([kr_ms, kr_ms], axis=1)
    for pr in range(MLA_HEADS // 2):
        psl = slice(2 * pr * HEAD_W, (2 * pr + 2) * HEAD_W)
        rq2 = lax.rsqrt(jnp.dot(q2[:, psl], g2, preferred_element_type=_F32) + NORM_EPS)
        rk2 = lax.rsqrt(jnp.dot(k2[:, psl], g2, preferred_element_type=_F32) + kr_ms2 + NORM_EPS)
        for hh in range(2):
            sl = slice((2 * pr + hh) * HEAD_W, (2 * pr + hh + 1) * HEAD_W)
            rsl = slice(hh * HEAD_W, (hh + 1) * HEAD_W)
            q_ref[0, :, sl] = (rq2[:, rsl] * (qraw[:, sl] * q_cos + qrot[:, sl] * q_sin) + one_q).astype(_BF16)
            k_ref[0, :, sl] = (rk2[:, rsl] * (kraw[:, sl] * gkv + k_rope) + kbias).astype(_BF16)


def _attn_kernel(q_ref, k_ref, vt_ref, ot_ref, acc_scr, m_scr, s_scr):
    i = pl.program_id(1)
    m_scr[...] = jnp.full(m_scr.shape, MASK_VALUE, _F32)
    acc_scr[...] = jnp.zeros(acc_scr.shape, _F32)
    causal = (lax.broadcasted_iota(jnp.int32, (TQ, 1), 0) <= lax.broadcasted_iota(jnp.int32, (1, TQ), 1))
    heads = [slice(h * HEAD_W, (h + 1) * HEAD_W) for h in range(MLA_HEADS)]

    def scores(j, h):
        start = pl.multiple_of(j * TQ, TQ)
        return lax.dot_general(k_ref[0, pl.ds(start, TQ), heads[h]], q_ref[0, :, heads[h]], _NT_DIMS,
                               preferred_element_type=_F32)

    def consume(j, h, s, diagonal):
        if diagonal:
            s = jnp.where(causal, s, MASK_VALUE)
        m_prev = m_scr[h]
        m_new = jnp.maximum(m_prev, jnp.max(s, axis=0, keepdims=True))
        p = jnp.exp2((s - m_new) * EXP2_SCALE)
        alpha = jnp.exp2((m_prev - m_new) * EXP2_SCALE)
        pv = jnp.dot(vt_ref[0, j, heads[h], :], p.astype(_BF16), preferred_element_type=_F32)
        acc_scr[h] = acc_scr[h] * alpha + pv
        m_scr[h] = m_new

    for h in range(MLA_HEADS):
        s_scr[h] = scores(0, h)

    def body(j, carry):
        for h in range(MLA_HEADS):
            s_old = s_scr[h]
            s_scr[h] = scores(j + 1, h)
            consume(j, h, s_old, False)
        return carry

    lax.fori_loop(0, i, body, 0)
    for h in range(MLA_HEADS):
        consume(i, h, s_scr[h], True)
    for h in range(MLA_HEADS):
        a = acc_scr[h]
        ot_ref[0, h * MLA_V:(h + 1) * MLA_V, :] = (a[0:MLA_V] * (1.0 / a[MLA_V:HEAD_W])).astype(_BF16)


def _post_kernel(first_layer, last_layer, *refs):
    if first_layer:
        x_ref, meta_ref = refs[:2]
        refs = refs[2:]
    else:
        x_ref, meta_ref = refs[0], None
        refs = refs[1:]
    ycr_ref, ot_ref, w_out, g_mlp, w1, w2, out_ref = refs
    if last_layer:
        x = x_ref[0]
    else:
        x = _load_tile(first_layer, pl.program_id(1), x_ref, meta_ref)
    x1 = (x + jnp.dot(ycr_ref[0], w_out[0:MIX_HALF, :], preferred_element_type=_F32)
          + lax.dot_general(ot_ref[0], w_out[MIX_HALF:, :], _TN_DIMS, preferred_element_type=_F32))
    hn = (_rms(x1, D_MODEL) * g_mlp[...]).astype(_BF16)
    up = jnp.dot(hn, w1[...], preferred_element_type=_F32)
    act = jnp.square(jnp.maximum(up, 0.0)).astype(_BF16)
    out_ref[0] = x1 + jnp.dot(act, w2[...], preferred_element_type=_F32)


def _const_spec(shape):
    nd = len(shape)
    return pl.BlockSpec(shape, lambda b, i, _nd=nd: (0,) * _nd, pipeline_mode=pl.Buffered(1))


def _params():
    return pltpu.CompilerParams(dimension_semantics=("parallel", "arbitrary"),
                                vmem_limit_bytes=VMEM_LIMIT)


def _rope_partner(lh):
    half = MLA_ROPE // 2
    return jnp.where((lh >= MLA_NOPE) & (lh < MLA_NOPE + half), lh + half,
                     jnp.where((lh >= MLA_NOPE + half) & (lh < MLA_QK), lh - half, lh))


def _tables():
    pos = jnp.maximum(jnp.arange(TP, dtype=_F32) - PAD, 0.0)
    inv_r = 1.0 / (ROPE_BASE ** (jnp.arange(0, RET_DK, 2, dtype=_F32) / RET_DK))
    ang_r = pos[:, None] * inv_r[None, :]
    l = jnp.arange(RET_W)
    cosr = jnp.cos(ang_r)[:, l % (RET_DK // 2)]
    sinr = jnp.sin(ang_r)[:, l % (RET_DK // 2)] * jnp.where((l % RET_DK) < RET_DK // 2, -1.0, 1.0)[None, :]
    inv_m = 1.0 / (ROPE_BASE ** (jnp.arange(0, MLA_ROPE, 2, dtype=_F32) / MLA_ROPE))
    ang_m = pos[:, None] * inv_m[None, :]
    lh = jnp.arange(HEAD_W)
    in_rope = (lh >= MLA_NOPE) & (lh < MLA_QK)
    fidx = jnp.clip(lh - MLA_NOPE, 0, MLA_ROPE - 1) % (MLA_ROPE // 2)
    cosm = jnp.where(in_rope[None, :], jnp.cos(ang_m)[:, fidx], 1.0)
    sgn = jnp.where(lh < MLA_NOPE + MLA_ROPE // 2, -1.0, 1.0)
    sinm = jnp.where(in_rope[None, :], jnp.sin(ang_m)[:, fidx] * sgn[None, :], 0.0)
    log_g = jnp.log1p(-jnp.exp2(-5.0 - jnp.arange(RET_HEADS, dtype=_F32)))
    idx = jnp.arange(TQ, dtype=_F32)
    diff = idx[:, None] - idx[None, :]
    dmat = jnp.where(diff >= 0, jnp.exp(jnp.maximum(diff, 0.0)[None] * log_g[:, None, None]), 0.0)
    lg_lane = log_g[l // RET_DK]
    xi = jnp.exp((idx[:, None] + 1.0) * lg_lane[None, :])
    zeta = jnp.exp((TQ - 1.0 - idx[:, None]) * lg_lane[None, :])
    cdec = jnp.exp(TQ * lg_lane)[None, :]
    same = (l[:, None] // RET_DK) == (l[None, :] // RET_DK)
    bdmask = same.astype(_F32)
    gmat = (same.astype(_F32) / RET_DK).astype(_BF16)
    l2 = jnp.arange(2 * HEAD_W)
    gpair = (((l2[:, None] // HEAD_W) == (l2[None, :] // HEAD_W)).astype(_F32) / MLA_QK).astype(_BF16)
    return dict(cosr=cosr, sinr=sinr, cosm=cosm, sinm=sinm, dmat=dmat, xi=xi, zeta=zeta, cdec=cdec,
                bdmask=bdmask, gmat=gmat, gpair=gpair)


def _layer_weights(l, attn_norm_g, w_in, conv_w, ret_gn_g, q_norm_g, w_uq, kv_norm_g, w_ukv,
                   q_head_norm_g, k_head_norm_g, w_out, mlp_norm_g, w_mlp_in, w_mlp_out):
    wi = w_in[l]
    w_in_ext = jnp.concatenate(
        [wi[:, :OFF_KR], jnp.zeros((D_MODEL, MLA_NOPE), _F32), wi[:, OFF_KR:],
         jnp.zeros((D_MODEL, HEAD_W - MLA_QK), _F32)], axis=1).astype(_BF16)
    partner = _rope_partner(jnp.arange(HEAD_W))
    is_rope = partner != jnp.arange(HEAD_W)
    wq = jnp.pad(w_uq[l].reshape(Q_LORA, MLA_HEADS, MLA_QK), ((0, 0), (0, 0), (0, HEAD_W - MLA_QK)))
    wq_rot = jnp.where(is_rope[None, None, :], wq[:, :, partner], 0.0)
    wkv = w_ukv[l].reshape(KV_LORA, MLA_HEADS, MLA_NOPE + MLA_V)
    w_k = jnp.pad(wkv[:, :, :MLA_NOPE], ((0, 0), (0, 0), (0, HEAD_W - MLA_NOPE))).reshape(KV_LORA, ATT_W)
    w_vt = jnp.pad(wkv[:, :, MLA_NOPE:], ((0, 0), (0, 0), (0, HEAD_W - MLA_V))).reshape(KV_LORA, ATT_W).T
    gq128 = jnp.pad(q_head_norm_g[l], (0, HEAD_W - MLA_QK))
    gk128 = jnp.pad(k_head_norm_g[l], (0, HEAD_W - MLA_QK))
    return dict(
        g_attn=attn_norm_g[l][None, :], w_in=w_in_ext, conv_w=conv_w[l], ret_g=ret_gn_g[l].reshape(1, RET_W),
        qn_g=q_norm_g[l][None, :], w_uq=wq.reshape(Q_LORA, ATT_W).astype(_BF16),
        w_uq_rot=wq_rot.reshape(Q_LORA, ATT_W).astype(_BF16), kvn_g=kv_norm_g[l][None, :],
        w_k=w_k.astype(_BF16), w_vt=w_vt.astype(_BF16), gq=gq128[None, :],
        gq_rot=jnp.where(is_rope, gq128[partner], 0.0)[None, :], gk=gk128[None, :],
        w_out=w_out[l].astype(_BF16), g_mlp=mlp_norm_g[l][None, :], w1=w_mlp_in[l].astype(_BF16),
        w2=w_mlp_out[l].astype(_BF16))


def _x_specs(first_layer):
    if first_layer:
        return [pl.BlockSpec((1, TQ, D_MODEL), lambda b, i: (b, jnp.maximum(i - 1, 0), 0)),
                _const_spec((TQ, D_MODEL))]
    return [pl.BlockSpec((1, TQ, D_MODEL), lambda b, i: (b, i, 0))]


def _pre_call(first_layer, batch, x_args, w, t):
    tile = lambda width: pl.BlockSpec((1, TQ, width), lambda b, i: (b, i, 0))
    row_tab = lambda width: pl.BlockSpec((TQ, width), lambda b, i: (i, 0))
    const_names = ["g_attn", "w_in", "conv_w"]
    in_specs = _x_specs(first_layer) + [_const_spec(w[n].shape) for n in const_names]
    args = list(x_args) + [w[n] for n in const_names]
    in_specs += [row_tab(RET_W), row_tab(RET_W)]
    args += [t["cosr"], t["sinr"]]
    for n, src in [("dmat", t), ("xi", t), ("zeta", t), ("cdec", t), ("bdmask", t), ("gmat", t), ("ret_g", w),
                   ("qn_g", w), ("w_uq", w), ("w_uq_rot", w), ("kvn_g", w), ("w_k", w), ("w_vt", w),
                   ("gpair", t), ("gq", w), ("gq_rot", w), ("gk", w)]:
        in_specs.append(_const_spec(src[n].shape))
        args.append(src[n])
    in_specs += [row_tab(HEAD_W), row_tab(HEAD_W)]
    args += [t["cosm"], t["sinm"]]
    out_shape = [jax.ShapeDtypeStruct((batch, TP, MIX_HALF), _BF16),
                 jax.ShapeDtypeStruct((batch, TP, ATT_W), _BF16),
                 jax.ShapeDtypeStruct((batch, TP, ATT_W), _BF16),
                 jax.ShapeDtypeStruct((batch, NT, ATT_W, TQ), _BF16)]
    return pl.pallas_call(
        functools.partial(_pre_kernel, first_layer),
        grid=(batch, NT), in_specs=in_specs,
        out_specs=[tile(MIX_HALF), tile(ATT_W), tile(ATT_W),
                   pl.BlockSpec((1, 1, ATT_W, TQ), lambda b, i: (b, i, 0, 0))],
        out_shape=out_shape,
        scratch_shapes=[pltpu.VMEM((TQ + 8, CONV_W), _F32), pltpu.VMEM((RET_W, RET_W), _F32)],
        compiler_params=_params(), name="pre")(*args)


def _attn_call(batch, q, k, vt):
    return pl.pallas_call(
        _attn_kernel, grid=(batch, NT),
        in_specs=[pl.BlockSpec((1, TQ, ATT_W), lambda b, i: (b, i, 0)),
                  pl.BlockSpec((1, TP, ATT_W), lambda b, i: (b, 0, 0)),
                  pl.BlockSpec((1, NT, ATT_W, TQ), lambda b, i: (b, 0, 0, 0))],
        out_specs=pl.BlockSpec((1, O_W, TQ), lambda b, i: (b, 0, i)),
        out_shape=jax.ShapeDtypeStruct((batch, O_W, TP), _BF16),
        scratch_shapes=[pltpu.VMEM((MLA_HEADS, HEAD_W, TQ), _F32), pltpu.VMEM((MLA_HEADS, 1, TQ), _F32),
                        pltpu.VMEM((MLA_HEADS, TQ, TQ), _F32)],
        compiler_params=_params(), name="attn")(q, k, vt)


def _post_call(first_layer, last_layer, batch, x_args, ycr, ot, w):
    off = 1 if last_layer else 0
    if last_layer:
        x_specs = [pl.BlockSpec((1, TQ, D_MODEL), lambda b, i: (b, i + off, 0))]
        out_shape = jax.ShapeDtypeStruct((batch, SEQ, D_MODEL), _F32)
        n_tiles = N_REAL_TILES
    else:
        x_specs = _x_specs(first_layer)
        out_shape = jax.ShapeDtypeStruct((batch, TP, D_MODEL), _F32)
        n_tiles = NT
    const_names = ["w_out", "g_mlp", "w1", "w2"]
    return pl.pallas_call(
        functools.partial(_post_kernel, first_layer, last_layer),
        grid=(batch, n_tiles),
        in_specs=x_specs + [pl.BlockSpec((1, TQ, MIX_HALF), lambda b, i: (b, i + off, 0)),
                            pl.BlockSpec((1, O_W, TQ), lambda b, i: (b, 0, i + off))]
        + [_const_spec(w[n].shape) for n in const_names],
        out_specs=pl.BlockSpec((1, TQ, D_MODEL), lambda b, i: (b, i, 0)),
        out_shape=out_shape, compiler_params=_params(), name="post")(
            *x_args, ycr, ot, *[w[n] for n in const_names])


def kernel(x, meta_tokens, attn_norm_g, w_in, conv_w, ret_gn_g, q_norm_g, w_uq, kv_norm_g, w_ukv,
           q_head_norm_g, k_head_norm_g, w_out, mlp_norm_g, w_mlp_in, w_mlp_out):
    batch = x.shape[0]
    depth = w_in.shape[0]
    assert x.shape[1:] == (SEQ, D_MODEL) and depth >= 2
    t = _tables()
    meta_pad = jnp.concatenate([jnp.zeros((PAD, D_MODEL), x.dtype), meta_tokens.astype(x.dtype)], axis=0)
    h = None
    for l in range(depth):
        w = _layer_weights(l, attn_norm_g, w_in, conv_w, ret_gn_g, q_norm_g, w_uq, kv_norm_g, w_ukv,
                           q_head_norm_g, k_head_norm_g, w_out, mlp_norm_g, w_mlp_in, w_mlp_out)
        first, last = l == 0, l == depth - 1
        x_args = (x, meta_pad) if first else (h,)
        ycr, q, k, vt = _pre_call(first, batch, x_args, w, t)
        ot = _attn_call(batch, q, k, vt)
        h = _post_call(first, last, batch, x_args, ycr, ot, w)
    return h
```

```python
import functools
import math

import jax
import jax.numpy as jnp
from jax import lax
from jax.experimental import pallas as pl
from jax.experimental.pallas import tpu as pltpu

D_MODEL = 1024
SEQ = 2048
N_META = 16
CONV_W = 256
CONV_K = 3
RET_HEADS = 4
RET_DK = 64
RET_W = RET_HEADS * RET_DK
MLA_HEADS = 8
MLA_NOPE = 64
MLA_ROPE = 32
MLA_V = 64
MLA_QK = MLA_NOPE + MLA_ROPE
Q_LORA = 256
KV_LORA = 128
D_FF = 4 * D_MODEL
ROPE_BASE = 10000.0
NORM_EPS = 1e-6

LANES = 128
BF16_ROWS = 16
HEAD_W = LANES
ATT_W = MLA_HEADS * HEAD_W
VT_ROWS = MLA_V + BF16_ROWS
VT_W = MLA_HEADS * VT_ROWS
MIX_HALF = CONV_W + RET_W
O_W = MLA_HEADS * MLA_V

TQ = 256
PAD = TQ - N_META
TP = SEQ + TQ
NT = TP // TQ
N_REAL_TILES = SEQ // TQ

OFF_CB, OFF_CC, OFF_CH = 0, 256, 512
OFF_RQ, OFF_RK, OFF_RV, OFF_RG = 768, 1024, 1280, 1536
OFF_CQ, OFF_CKV, OFF_KR = 1792, 2048, 2176
D_IN_EXT = OFF_KR + HEAD_W

KV_UNROLL = 4
MASK_VALUE = -1e30
EXP2_SCALE = math.log2(math.e) / math.sqrt(MLA_QK)
VMEM_LIMIT = 56 * 1024 * 1024

_F32 = jnp.float32
_BF16 = jnp.bfloat16
_NT_DIMS = (((1,), (1,)), ((), ()))
_TN_DIMS = (((0,), (0,)), ((), ()))


def _rms(x, n):
    return x * lax.rsqrt(jnp.sum(x * x, axis=-1, keepdims=True) * (1.0 / n) + NORM_EPS)


def _rot_pairs(z, half, first):
    w = z.shape[-1]
    return jnp.where(first, pltpu.roll(z, w - half, axis=1), pltpu.roll(z, half, axis=1))


def _load_tile(first_layer, i, x_ref, meta_ref):
    x = x_ref[0]
    if first_layer:
        return jnp.where(i == 0, meta_ref[...], x)
    row = lax.broadcasted_iota(jnp.int32, (TQ, 1), 0)
    return jnp.where((i > 0) | (row >= PAD), x, 0.0)


def _pre_kernel(first_layer, *refs):
    if first_layer:
        x_ref, meta_ref = refs[:2]
        refs = refs[2:]
    else:
        x_ref, meta_ref = refs[0], None
        refs = refs[1:]
    (g_attn, w_in, conv_w, cosr, sinr, dmat, xi, zeta, cdec, bdmask, gmat, ret_g,
     qn_g, w_uq, w_uq_rot, kvn_g, w_k, w_vt, gpair, gq, gq_rot, gk, cosm, sinm,
     ycr_ref, q_ref, k_ref, vt_ref,
     ubuf, state, ycr0, q0, k0, vt0, halo0, state0) = refs
    b = pl.program_id(0)
    i = pl.program_id(1)
    fresh = (b == 0) | (i > 0)

    @pl.when((b == 0) & (i == 0))
    def _():
        ubuf[0:8, :] = jnp.zeros((8, CONV_W), _F32)
        state[...] = jnp.zeros_like(state)

    @pl.when(fresh)
    def _():
        _pre_tile(first_layer, i, x_ref, meta_ref, g_attn, w_in, conv_w, cosr, sinr, dmat, xi, zeta, cdec,
                  bdmask, gmat, ret_g, qn_g, w_uq, w_uq_rot, kvn_g, w_k, w_vt, gpair, gq, gq_rot, gk, cosm,
                  sinm, ycr_ref, q_ref, k_ref, vt_ref, ubuf, state)

    @pl.when((b == 0) & (i == 0))
    def _():
        ycr0[...] = ycr_ref[0]
        q0[...] = q_ref[0]
        k0[...] = k_ref[0]
        vt0[...] = vt_ref[0, 0]
        halo0[...] = ubuf[0:8, :]
        state0[...] = state[...]

    @pl.when(jnp.logical_not(fresh))
    def _():
        ycr_ref[0] = ycr0[...]
        q_ref[0] = q0[...]
        k_ref[0] = k0[...]
        vt_ref[0, 0] = vt0[...]
        ubuf[0:8, :] = halo0[...]
        state[...] = state0[...]


def _pre_tile(first_layer, i, x_ref, meta_ref, g_attn, w_in, conv_w, cosr, sinr, dmat, xi, zeta, cdec, bdmask,
              gmat, ret_g, qn_g, w_uq, w_uq_rot, kvn_g, w_k, w_vt, gpair, gq, gq_rot, gk, cosm, sinm,
              ycr_ref, q_ref, k_ref, vt_ref, ubuf, state):
    x = _load_tile(first_layer, i, x_ref, meta_ref)
    hn = (_rms(x, D_MODEL) * g_attn[...]).astype(_BF16)
    proj = jnp.dot(hn, w_in[...], preferred_element_type=_F32)

    cb = proj[:, OFF_CB:OFF_CB + CONV_W]
    u = proj[:, OFF_CC:OFF_CC + CONV_W] * proj[:, OFF_CH:OFF_CH + CONV_W]
    ubuf[8:8 + TQ, :] = u
    cw = conv_w[...]
    conv = cw[0:1] * ubuf[6:6 + TQ, :] + cw[1:2] * ubuf[7:7 + TQ, :] + cw[2:3] * u
    ubuf[0:8, :] = u[TQ - 8:, :]
    ycr_ref[0, :, 0:CONV_W] = (cb * conv).astype(_BF16)

    lane = lax.broadcasted_iota(jnp.int32, (1, RET_W), 1)
    first = (lane % RET_DK) < (RET_DK // 2)
    rq = proj[:, OFF_RQ:OFF_RQ + RET_W]
    rk = proj[:, OFF_RK:OFF_RK + RET_W]
    rv = proj[:, OFF_RV:OFF_RV + RET_W]
    rg = proj[:, OFF_RG:OFF_RG + RET_W]
    cr, sr = cosr[...], sinr[...]
    qr = rq * cr + _rot_pairs(rq, RET_DK // 2, first) * sr
    kr = (rk * cr + _rot_pairs(rk, RET_DK // 2, first) * sr) * (RET_DK ** -0.5)
    kb = kr.astype(_BF16)
    inner = jnp.zeros((TQ, RET_W), _F32)
    for h in range(RET_HEADS):
        hmask = (lane // RET_DK) == h
        qh = jnp.where(hmask, qr, 0.0).astype(_BF16)
        sc = lax.dot_general(qh, kb, _NT_DIMS, preferred_element_type=_F32)
        sc = (sc * dmat[h]).astype(_BF16)
        vh = jnp.where(hmask, rv, 0.0).astype(_BF16)
        inner = inner + jnp.dot(sc, vh, preferred_element_type=_F32)
    st = state[...]
    cross = jnp.dot(qr.astype(_BF16), st.astype(_BF16), preferred_element_type=_F32) * xi[...]
    upd = lax.dot_general((kr * zeta[...]).astype(_BF16), rv.astype(_BF16), _TN_DIMS,
                          preferred_element_type=_F32)
    state[...] = cdec[...] * st + bdmask[...] * upd
    o = inner + cross
    msq = jnp.dot((o * o).astype(_BF16), gmat[...], preferred_element_type=_F32)
    o_n = o * lax.rsqrt(msq + NORM_EPS) * ret_g[...]
    gate = rg * (1.0 / (1.0 + jnp.exp(-rg)))
    ycr_ref[0, :, CONV_W:MIX_HALF] = (gate * o_n).astype(_BF16)

    cq = proj[:, OFF_CQ:OFF_CQ + Q_LORA]
    ckv = proj[:, OFF_CKV:OFF_CKV + KV_LORA]
    kr128 = proj[:, OFF_KR:OFF_KR + HEAD_W]
    cqn = (_rms(cq, Q_LORA) * qn_g[...]).astype(_BF16)
    ckvn = (_rms(ckv, KV_LORA) * kvn_g[...]).astype(_BF16)
    qraw = jnp.dot(cqn, w_uq[...], preferred_element_type=_F32)
    qrot = jnp.dot(cqn, w_uq_rot[...], preferred_element_type=_F32)
    kraw = jnp.dot(ckvn, w_k[...], preferred_element_type=_F32)
    vt = lax.dot_general(w_vt[...], ckvn, _NT_DIMS, preferred_element_type=_F32)
    vrow = lax.broadcasted_iota(jnp.int32, (VT_W, 1), 0)
    vt_ref[0, 0] = jnp.where((vrow % VT_ROWS) >= MLA_V, 1.0, vt).astype(_BF16)

    lane_h = lax.broadcasted_iota(jnp.int32, (1, HEAD_W), 1)
    first_m = lane_h < (MLA_NOPE + MLA_ROPE // 2)
    cm, sm = cosm[...], sinm[...]
    gkv = gk[...]
    q_cos, q_sin = gq[...] * cm, gq_rot[...] * sm
    krg = kr128 * gkv
    k_rope = krg * cm + _rot_pairs(krg, MLA_ROPE // 2, first_m) * sm
    one_q = jnp.where(lane_h == MLA_QK, 1.0, 0.0)
    rowg = i * TQ + lax.broadcasted_iota(jnp.int32, (TQ, 1), 0)
    kbias = jnp.where((rowg < PAD) & (lane_h == MLA_QK), MASK_VALUE, 0.0)
    g2 = gpair[...]
    q2 = (qraw * qraw).astype(_BF16)
    k2 = (kraw * kraw).astype(_BF16)
    kr_ms = jnp.dot((kr128 * kr128).astype(_BF16), g2[0:HEAD_W, 0:HEAD_W], preferred_element_type=_F32)
    kr_ms2 = jnp.concatenate([kr_ms, kr_ms], axis=1)
    for pr in range(MLA_HEADS // 2):
        psl = slice(2 * pr * HEAD_W, (2 * pr + 2) * HEAD_W)
        rq2 = lax.rsqrt(jnp.dot(q2[:, psl], g2, preferred_element_type=_F32) + NORM_EPS)
        rk2 = lax.rsqrt(jnp.dot(k2[:, psl], g2, preferred_element_type=_F32) + kr_ms2 + NORM_EPS)
        for hh in range(2):
            sl = slice((2 * pr + hh) * HEAD_W, (2 * pr + hh + 1) * HEAD_W)
            rsl = slice(hh * HEAD_W, (hh + 1) * HEAD_W)
            q_ref[0, :, sl] = (rq2[:, rsl] * (qraw[:, sl] * q_cos + qrot[:, sl] * q_sin) + one_q).astype(_BF16)
            k_ref[0, :, sl] = (rk2[:, rsl] * (kraw[:, sl] * gkv + k_rope) + kbias).astype(_BF16)


def _attn_kernel(q_ref, k_ref, vt_ref, ot_ref, acc_scr, m_scr, s_scr):
    causal = (lax.broadcasted_iota(jnp.int32, (TQ, 1), 0) <= lax.broadcasted_iota(jnp.int32, (1, TQ), 1))
    heads = [slice(h * HEAD_W, (h + 1) * HEAD_W) for h in range(MLA_HEADS)]
    vheads = [slice(h * VT_ROWS, (h + 1) * VT_ROWS) for h in range(MLA_HEADS)]

    def tile_start(t):
        return t * TQ if isinstance(t, int) else pl.multiple_of(t * TQ, TQ)

    def scores(i, j, h):
        qs, ks = tile_start(i), tile_start(j)
        return lax.dot_general(k_ref[0, pl.ds(ks, TQ), heads[h]], q_ref[0, pl.ds(qs, TQ), heads[h]], _NT_DIMS,
                               preferred_element_type=_F32)

    def stage(i, j, nxt, diagonal):
        for h in range(MLA_HEADS):
            s = s_scr[h]
            s_scr[h] = scores(nxt[0], nxt[1], h)
            if diagonal:
                s = jnp.where(causal, s, MASK_VALUE)
            m_prev = m_scr[h]
            m_new = jnp.maximum(m_prev, jnp.max(s, axis=0, keepdims=True))
            p = jnp.exp2(s - m_new)
            alpha = jnp.exp2(m_prev - m_new)
            pv = jnp.dot(vt_ref[0, j, vheads[h], :], p.astype(_BF16), preferred_element_type=_F32)
            acc_scr[h] = acc_scr[h] * alpha + pv
            m_scr[h] = m_new

    for h in range(MLA_HEADS):
        s_scr[h] = scores(0, 0, h)

    def tile_body(i, carry):
        m_scr[...] = jnp.full(m_scr.shape, MASK_VALUE, _F32)
        acc_scr[...] = jnp.zeros(acc_scr.shape, _F32)

        def run(j0, n):
            for u in range(n):
                stage(i, j0 + u, (i, j0 + u + 1), False)

        def unrolled_body(jj, c):
            run(jj * KV_UNROLL, KV_UNROLL)
            return c

        lax.fori_loop(0, i // KV_UNROLL, unrolled_body, 0)
        done = (i // KV_UNROLL) * KV_UNROLL
        size = KV_UNROLL // 2
        while size >= 1:
            take = (i & size) != 0
            pl.when(take)(functools.partial(run, done, size))
            done = done + jnp.where(take, size, 0)
            size //= 2
        stage(i, i, (jnp.minimum(i + 1, NT - 1), 0), True)
        for h in range(MLA_HEADS):
            a = acc_scr[h]
            inv = 1.0 / a[MLA_V:MLA_V + 8]
            ot_ref[0, i, h * MLA_V:(h + 1) * MLA_V, :] = (
                a[0:MLA_V] * jnp.concatenate([inv] * (MLA_V // 8), axis=0)).astype(_BF16)
        return carry

    lax.fori_loop(0, NT, tile_body, 0)


def _post_tile(first_layer, last_layer, x_ref, meta_ref, ycr_ref, ot_ref, w_out, g_mlp, w1, w2, out_ref):
    if last_layer:
        x = x_ref[0]
    else:
        x = _load_tile(first_layer, pl.program_id(1), x_ref, meta_ref)
    x1 = (x + jnp.dot(ycr_ref[0], w_out[0:MIX_HALF, :], preferred_element_type=_F32)
          + lax.dot_general(ot_ref[0, 0], w_out[MIX_HALF:, :], _TN_DIMS, preferred_element_type=_F32))
    hn = (_rms(x1, D_MODEL) * g_mlp[...]).astype(_BF16)
    up = jnp.dot(hn, w1[...], preferred_element_type=_F32)
    act = jnp.square(jnp.maximum(up, 0.0)).astype(_BF16)
    out_ref[0] = x1 + jnp.dot(act, w2[...], preferred_element_type=_F32)


def _post_kernel(first_layer, last_layer, *refs):
    if first_layer:
        x_ref, meta_ref = refs[:2]
        refs = refs[2:]
    else:
        x_ref, meta_ref = refs[0], None
        refs = refs[1:]
    if last_layer:
        ycr_ref, ot_ref, w_out, g_mlp, w1, w2, out_ref = refs
        _post_tile(first_layer, last_layer, x_ref, meta_ref, ycr_ref, ot_ref, w_out, g_mlp, w1, w2, out_ref)
        return
    ycr_ref, ot_ref, w_out, g_mlp, w1, w2, out_ref, h0 = refs
    b = pl.program_id(0)
    i = pl.program_id(1)
    fresh = (b == 0) | (i > 0)

    @pl.when(fresh)
    def _():
        _post_tile(first_layer, last_layer, x_ref, meta_ref, ycr_ref, ot_ref, w_out, g_mlp, w1, w2, out_ref)

    @pl.when((b == 0) & (i == 0))
    def _():
        h0[...] = out_ref[0]

    @pl.when(jnp.logical_not(fresh))
    def _():
        out_ref[0] = h0[...]


def _const_spec(shape):
    nd = len(shape)
    return pl.BlockSpec(shape, lambda b, i, _nd=nd: (0,) * _nd, pipeline_mode=pl.Buffered(1))


def _params(n_axes=2):
    return pltpu.CompilerParams(dimension_semantics=("arbitrary",) * n_axes, vmem_limit_bytes=VMEM_LIMIT)


def _rope_partner(lh):
    half = MLA_ROPE // 2
    return jnp.where((lh >= MLA_NOPE) & (lh < MLA_NOPE + half), lh + half,
                     jnp.where((lh >= MLA_NOPE + half) & (lh < MLA_QK), lh - half, lh))


def _tables():
    pos = jnp.maximum(jnp.arange(TP, dtype=_F32) - PAD, 0.0)
    inv_r = 1.0 / (ROPE_BASE ** (jnp.arange(0, RET_DK, 2, dtype=_F32) / RET_DK))
    ang_r = pos[:, None] * inv_r[None, :]
    l = jnp.arange(RET_W)
    cosr = jnp.cos(ang_r)[:, l % (RET_DK // 2)]
    sinr = jnp.sin(ang_r)[:, l % (RET_DK // 2)] * jnp.where((l % RET_DK) < RET_DK // 2, -1.0, 1.0)[None, :]
    inv_m = 1.0 / (ROPE_BASE ** (jnp.arange(0, MLA_ROPE, 2, dtype=_F32) / MLA_ROPE))
    ang_m = pos[:, None] * inv_m[None, :]
    lh = jnp.arange(HEAD_W)
    in_rope = (lh >= MLA_NOPE) & (lh < MLA_QK)
    fidx = jnp.clip(lh - MLA_NOPE, 0, MLA_ROPE - 1) % (MLA_ROPE // 2)
    cosm = jnp.where(in_rope[None, :], jnp.cos(ang_m)[:, fidx], 1.0)
    sgn = jnp.where(lh < MLA_NOPE + MLA_ROPE // 2, -1.0, 1.0)
    sinm = jnp.where(in_rope[None, :], jnp.sin(ang_m)[:, fidx] * sgn[None, :], 0.0)
    log_g = jnp.log1p(-jnp.exp2(-5.0 - jnp.arange(RET_HEADS, dtype=_F32)))
    idx = jnp.arange(TQ, dtype=_F32)
    diff = idx[:, None] - idx[None, :]
    dmat = jnp.where(diff >= 0, jnp.exp(jnp.maximum(diff, 0.0)[None] * log_g[:, None, None]), 0.0)
    lg_lane = log_g[l // RET_DK]
    xi = jnp.exp((idx[:, None] + 1.0) * lg_lane[None, :])
    zeta = jnp.exp((TQ - 1.0 - idx[:, None]) * lg_lane[None, :])
    cdec = jnp.exp(TQ * lg_lane)[None, :]
    same = (l[:, None] // RET_DK) == (l[None, :] // RET_DK)
    bdmask = same.astype(_F32)
    gmat = (same.astype(_F32) / RET_DK).astype(_BF16)
    l2 = jnp.arange(2 * HEAD_W)
    gpair = (((l2[:, None] // HEAD_W) == (l2[None, :] // HEAD_W)).astype(_F32) / MLA_QK).astype(_BF16)
    return dict(cosr=cosr, sinr=sinr, cosm=cosm, sinm=sinm, dmat=dmat, xi=xi, zeta=zeta, cdec=cdec,
                bdmask=bdmask, gmat=gmat, gpair=gpair)


def _layer_weights(l, attn_norm_g, w_in, conv_w, ret_gn_g, q_norm_g, w_uq, kv_norm_g, w_ukv,
                   q_head_norm_g, k_head_norm_g, w_out, mlp_norm_g, w_mlp_in, w_mlp_out):
    wi = w_in[l]
    w_in_ext = jnp.concatenate(
        [wi[:, :OFF_KR], jnp.zeros((D_MODEL, MLA_NOPE), _F32), wi[:, OFF_KR:],
         jnp.zeros((D_MODEL, HEAD_W - MLA_QK), _F32)], axis=1).astype(_BF16)
    partner = _rope_partner(jnp.arange(HEAD_W))
    is_rope = partner != jnp.arange(HEAD_W)
    wq = jnp.pad(w_uq[l].reshape(Q_LORA, MLA_HEADS, MLA_QK), ((0, 0), (0, 0), (0, HEAD_W - MLA_QK)))
    wq_rot = jnp.where(is_rope[None, None, :], wq[:, :, partner], 0.0)
    wkv = w_ukv[l].reshape(KV_LORA, MLA_HEADS, MLA_NOPE + MLA_V)
    w_k = jnp.pad(wkv[:, :, :MLA_NOPE], ((0, 0), (0, 0), (0, HEAD_W - MLA_NOPE))).reshape(KV_LORA, ATT_W)
    w_vt = jnp.pad(wkv[:, :, MLA_NOPE:], ((0, 0), (0, 0), (0, VT_ROWS - MLA_V))).reshape(KV_LORA, VT_W).T
    gq128 = jnp.pad(q_head_norm_g[l], (0, HEAD_W - MLA_QK)) * EXP2_SCALE
    gk128 = jnp.pad(k_head_norm_g[l], (0, HEAD_W - MLA_QK))
    return dict(
        g_attn=attn_norm_g[l][None, :], w_in=w_in_ext, conv_w=conv_w[l], ret_g=ret_gn_g[l].reshape(1, RET_W),
        qn_g=q_norm_g[l][None, :], w_uq=wq.reshape(Q_LORA, ATT_W).astype(_BF16),
        w_uq_rot=wq_rot.reshape(Q_LORA, ATT_W).astype(_BF16), kvn_g=kv_norm_g[l][None, :],
        w_k=w_k.astype(_BF16), w_vt=w_vt.astype(_BF16), gq=gq128[None, :],
        gq_rot=jnp.where(is_rope, gq128[partner], 0.0)[None, :], gk=gk128[None, :],
        w_out=w_out[l].astype(_BF16), g_mlp=mlp_norm_g[l][None, :], w1=w_mlp_in[l].astype(_BF16),
        w2=w_mlp_out[l].astype(_BF16))


def _x_specs(first_layer):
    if first_layer:
        return [pl.BlockSpec((1, TQ, D_MODEL), lambda b, i: (b, jnp.maximum(i - 1, 0), 0)),
                _const_spec((TQ, D_MODEL))]
    return [pl.BlockSpec((1, TQ, D_MODEL), lambda b, i: (b, i, 0))]


def _pre_call(first_layer, batch, x_args, w, t):
    tile = lambda width: pl.BlockSpec((1, TQ, width), lambda b, i: (b, i, 0))
    row_tab = lambda width: pl.BlockSpec((TQ, width), lambda b, i: (i, 0))
    const_names = ["g_attn", "w_in", "conv_w"]
    in_specs = _x_specs(first_layer) + [_const_spec(w[n].shape) for n in const_names]
    args = list(x_args) + [w[n] for n in const_names]
    in_specs += [row_tab(RET_W), row_tab(RET_W)]
    args += [t["cosr"], t["sinr"]]
    for n, src in [("dmat", t), ("xi", t), ("zeta", t), ("cdec", t), ("bdmask", t), ("gmat", t), ("ret_g", w),
                   ("qn_g", w), ("w_uq", w), ("w_uq_rot", w), ("kvn_g", w), ("w_k", w), ("w_vt", w),
                   ("gpair", t), ("gq", w), ("gq_rot", w), ("gk", w)]:
        in_specs.append(_const_spec(src[n].shape))
        args.append(src[n])
    in_specs += [row_tab(HEAD_W), row_tab(HEAD_W)]
    args += [t["cosm"], t["sinm"]]
    out_shape = [jax.ShapeDtypeStruct((batch, TP, MIX_HALF), _BF16),
                 jax.ShapeDtypeStruct((batch, TP, ATT_W), _BF16),
                 jax.ShapeDtypeStruct((batch, TP, ATT_W), _BF16),
                 jax.ShapeDtypeStruct((batch, NT, VT_W, TQ), _BF16)]
    scratch = [pltpu.VMEM((TQ + 8, CONV_W), _F32), pltpu.VMEM((RET_W, RET_W), _F32),
               pltpu.VMEM((TQ, MIX_HALF), _BF16), pltpu.VMEM((TQ, ATT_W), _BF16),
               pltpu.VMEM((TQ, ATT_W), _BF16), pltpu.VMEM((VT_W, TQ), _BF16),
               pltpu.VMEM((8, CONV_W), _F32), pltpu.VMEM((RET_W, RET_W), _F32)]
    return pl.pallas_call(
        functools.partial(_pre_kernel, first_layer),
        grid=(batch, NT), in_specs=in_specs,
        out_specs=[tile(MIX_HALF), tile(ATT_W), tile(ATT_W),
                   pl.BlockSpec((1, 1, VT_W, TQ), lambda b, i: (b, i, 0, 0))],
        out_shape=out_shape, scratch_shapes=scratch, compiler_params=_params(), name="pre")(*args)


def _attn_call(batch, q, k, vt):
    return pl.pallas_call(
        _attn_kernel, grid=(batch,),
        in_specs=[pl.BlockSpec((1, TP, ATT_W), lambda b: (b, 0, 0)),
                  pl.BlockSpec((1, TP, ATT_W), lambda b: (b, 0, 0)),
                  pl.BlockSpec((1, NT, VT_W, TQ), lambda b: (b, 0, 0, 0))],
        out_specs=pl.BlockSpec((1, NT, O_W, TQ), lambda b: (b, 0, 0, 0)),
        out_shape=jax.ShapeDtypeStruct((batch, NT, O_W, TQ), _BF16),
        scratch_shapes=[pltpu.VMEM((MLA_HEADS, VT_ROWS, TQ), _F32), pltpu.VMEM((MLA_HEADS, 1, TQ), _F32),
                        pltpu.VMEM((MLA_HEADS, TQ, TQ), _F32)],
        compiler_params=_params(1), name="attn")(q, k, vt)


def _post_call(first_layer, last_layer, batch, x_args, ycr, ot, w):
    off = 1 if last_layer else 0
    if last_layer:
        x_specs = [pl.BlockSpec((1, TQ, D_MODEL), lambda b, i: (b, i + off, 0))]
        out_shape = jax.ShapeDtypeStruct((batch, SEQ, D_MODEL), _F32)
        n_tiles = N_REAL_TILES
        scratch = []
    else:
        x_specs = _x_specs(first_layer)
        out_shape = jax.ShapeDtypeStruct((batch, TP, D_MODEL), _F32)
        n_tiles = NT
        scratch = [pltpu.VMEM((TQ, D_MODEL), _F32)]
    const_names = ["w_out", "g_mlp", "w1", "w2"]
    return pl.pallas_call(
        functools.partial(_post_kernel, first_layer, last_layer),
        grid=(batch, n_tiles),
        in_specs=x_specs + [pl.BlockSpec((1, TQ, MIX_HALF), lambda b, i: (b, i + off, 0)),
                            pl.BlockSpec((1, 1, O_W, TQ), lambda b, i: (b, i + off, 0, 0))]
        + [_const_spec(w[n].shape) for n in const_names],
        out_specs=pl.BlockSpec((1, TQ, D_MODEL), lambda b, i: (b, i, 0)),
        out_shape=out_shape, scratch_shapes=scratch, compiler_params=_params(), name="post")(
            *x_args, ycr, ot, *[w[n] for n in const_names])


def kernel(x, meta_tokens, attn_norm_g, w_in, conv_w, ret_gn_g, q_norm_g, w_uq, kv_norm_g, w_ukv,
           q_head_norm_g, k_head_norm_g, w_out, mlp_norm_g, w_mlp_in, w_mlp_out):
    batch = x.shape[0]
    depth = w_in.shape[0]
    assert x.shape[1:] == (SEQ, D_MODEL) and depth >= 2
    t = _tables()
    meta_pad = jnp.concatenate([jnp.zeros((PAD, D_MODEL), x.dtype), meta_tokens.astype(x.dtype)], axis=0)
    h = None
    for l in range(depth):
        w = _layer_weights(l, attn_norm_g, w_in, conv_w, ret_gn_g, q_norm_g, w_uq, kv_norm_g, w_ukv,
                           q_head_norm_g, k_head_norm_g, w_out, mlp_norm_g, w_mlp_in, w_mlp_out)
        first, last = l == 0, l == depth - 1
        x_args = (x, meta_pad) if first else (h,)
        ycr, q, k, vt = _pre_call(first, batch, x_args, w, t)
        ot = _attn_call(batch, q, k, vt)
        h = _post_call(first, last, batch, x_args, ycr, ot, w)
    return h
```

```python
import functools
import math

import jax
import jax.numpy as jnp
from jax import lax
from jax.experimental import pallas as pl
from jax.experimental.pallas import tpu as pltpu

D_MODEL = 1024
SEQ = 2048
N_META = 16
CONV_W = 256
CONV_K = 3
RET_HEADS = 4
RET_DK = 64
RET_W = RET_HEADS * RET_DK
MLA_HEADS = 8
MLA_NOPE = 64
MLA_ROPE = 32
MLA_V = 64
MLA_QK = MLA_NOPE + MLA_ROPE
Q_LORA = 256
KV_LORA = 128
D_FF = 4 * D_MODEL
ROPE_BASE = 10000.0
NORM_EPS = 1e-6

LANES = 128
BF16_ROWS = 16
HEAD_W = LANES
ATT_W = MLA_HEADS * HEAD_W
VT_ROWS = MLA_V + BF16_ROWS
VT_W = MLA_HEADS * VT_ROWS
MIX_HALF = CONV_W + RET_W
O_W = MLA_HEADS * MLA_V

TQ = 256
SUB = 2
STEP = SUB * TQ
PAD = STEP - N_META
TP = SEQ + STEP
NB = TP // TQ
NS = TP // STEP
FIRST_TILE = PAD // TQ

OFF_CB, OFF_CC, OFF_CH = 0, 256, 512
OFF_RQ, OFF_RK, OFF_RV, OFF_RG = 768, 1024, 1280, 1536
OFF_CQ, OFF_CKV, OFF_KR = 1792, 2048, 2176
D_IN_EXT = OFF_KR + HEAD_W

KV_UNROLL = 4
MASK_VALUE = -1e30
EXP2_SCALE = math.log2(math.e) / math.sqrt(MLA_QK)
VMEM_LIMIT = 56 * 1024 * 1024

_F32 = jnp.float32
_BF16 = jnp.bfloat16
_NT_DIMS = (((1,), (1,)), ((), ()))
_TN_DIMS = (((0,), (0,)), ((), ()))


def _rms(x, n):
    return x * lax.rsqrt(jnp.sum(x * x, axis=-1, keepdims=True) * (1.0 / n) + NORM_EPS)


def _rot_pairs(z, half, first):
    w = z.shape[-1]
    return jnp.where(first, pltpu.roll(z, w - half, axis=1), pltpu.roll(z, half, axis=1))


def _load_step(first_layer, i, x_ref, meta_ref):
    x = x_ref[0]
    if first_layer:
        return jnp.where(i == 0, meta_ref[...], x)
    row = lax.broadcasted_iota(jnp.int32, (STEP, 1), 0)
    return jnp.where((i > 0) | (row >= PAD), x, 0.0)


def _pre_kernel(first_layer, *refs):
    if first_layer:
        x_ref, meta_ref = refs[:2]
        refs = refs[2:]
    else:
        x_ref, meta_ref = refs[0], None
        refs = refs[1:]
    consts = refs[:24]
    ycr_ref, q_ref, k_ref, vt_ref, ubuf, state, ycr0, q0, k0, vt0, halo0, state0 = refs[24:]
    b = pl.program_id(0)
    i = pl.program_id(1)
    fresh = (b == 0) | (i > 0)

    @pl.when((b == 0) & (i == 0))
    def _():
        ubuf[0:8, :] = jnp.zeros((8, CONV_W), _F32)
        state[...] = jnp.zeros_like(state)

    @pl.when(fresh)
    def _():
        x = _load_step(first_layer, i, x_ref, meta_ref)
        for sub in range(SUB):
            _pre_tile(i, sub, x[sub * TQ:(sub + 1) * TQ], consts, ycr_ref, q_ref, k_ref, vt_ref, ubuf, state)

    @pl.when((b == 0) & (i == 0))
    def _():
        ycr0[...] = ycr_ref[0]
        q0[...] = q_ref[0]
        k0[...] = k_ref[0]
        vt0[...] = vt_ref[0]
        halo0[...] = ubuf[0:8, :]
        state0[...] = state[...]

    @pl.when(jnp.logical_not(fresh))
    def _():
        ycr_ref[0] = ycr0[...]
        q_ref[0] = q0[...]
        k_ref[0] = k0[...]
        vt_ref[0] = vt0[...]
        ubuf[0:8, :] = halo0[...]
        state[...] = state0[...]


def _pre_tile(i, sub, x, consts, ycr_ref, q_ref, k_ref, vt_ref, ubuf, state):
    (g_attn, w_in, conv_w, cosr, sinr, dmat, xi, zeta, cdec, bdmask, gmat, ret_g,
     qn_g, w_uq, w_uq_rot, kvn_g, w_k, w_vt, gpair, gq, gq_rot, gk, cosm, sinm) = consts
    rows = slice(sub * TQ, (sub + 1) * TQ)
    hn = (_rms(x, D_MODEL) * g_attn[...]).astype(_BF16)
    proj_m = jnp.dot(hn, w_in[:, OFF_CQ:D_IN_EXT], preferred_element_type=_F32)
    proj_r = jnp.dot(hn, w_in[:, OFF_RQ:OFF_CQ], preferred_element_type=_F32)
    proj_c = jnp.dot(hn, w_in[:, OFF_CB:OFF_RQ], preferred_element_type=_F32)

    cq = proj_m[:, 0:Q_LORA]
    ckv = proj_m[:, Q_LORA:Q_LORA + KV_LORA]
    kr128 = proj_m[:, Q_LORA + KV_LORA:]
    cqn = (_rms(cq, Q_LORA) * qn_g[...]).astype(_BF16)
    ckvn = (_rms(ckv, KV_LORA) * kvn_g[...]).astype(_BF16)
    qraw = jnp.dot(cqn, w_uq[...], preferred_element_type=_F32)
    qrot = jnp.dot(cqn, w_uq_rot[...], preferred_element_type=_F32)
    kraw = jnp.dot(ckvn, w_k[...], preferred_element_type=_F32)
    vt = lax.dot_general(w_vt[...], ckvn, _NT_DIMS, preferred_element_type=_F32)
    vrow = lax.broadcasted_iota(jnp.int32, (VT_W, 1), 0)
    vt_ref[0, sub] = jnp.where((vrow % VT_ROWS) >= MLA_V, 1.0, vt).astype(_BF16)

    lane = lax.broadcasted_iota(jnp.int32, (1, RET_W), 1)
    first = (lane % RET_DK) < (RET_DK // 2)
    rq = proj_r[:, 0:RET_W]
    rk = proj_r[:, RET_W:2 * RET_W]
    rv = proj_r[:, 2 * RET_W:3 * RET_W]
    rg = proj_r[:, 3 * RET_W:]
    cr, sr = cosr[rows, :], sinr[rows, :]
    qr = rq * cr + _rot_pairs(rq, RET_DK // 2, first) * sr
    kr = (rk * cr + _rot_pairs(rk, RET_DK // 2, first) * sr) * (RET_DK ** -0.5)
    kb = kr.astype(_BF16)
    hmasks = [(lane // RET_DK) == h for h in range(RET_HEADS)]
    scs = [lax.dot_general(jnp.where(hmasks[h], qr, 0.0).astype(_BF16), kb, _NT_DIMS,
                           preferred_element_type=_F32) for h in range(RET_HEADS)]

    g2 = gpair[...]
    q2 = (qraw * qraw).astype(_BF16)
    k2 = (kraw * kraw).astype(_BF16)
    kr_ms = jnp.dot((kr128 * kr128).astype(_BF16), g2[0:HEAD_W, 0:HEAD_W], preferred_element_type=_F32)
    kr_ms2 = jnp.concatenate([kr_ms, kr_ms], axis=1)
    rq2, rk2 = [], []
    for pr in range(MLA_HEADS // 2):
        psl = slice(2 * pr * HEAD_W, (2 * pr + 2) * HEAD_W)
        rq2.append(lax.rsqrt(jnp.dot(q2[:, psl], g2, preferred_element_type=_F32) + NORM_EPS))
        rk2.append(lax.rsqrt(jnp.dot(k2[:, psl], g2, preferred_element_type=_F32) + kr_ms2 + NORM_EPS))

    inner = jnp.zeros((TQ, RET_W), _F32)
    for h in range(RET_HEADS):
        sc = (scs[h] * dmat[h]).astype(_BF16)
        vh = jnp.where(hmasks[h], rv, 0.0).astype(_BF16)
        inner = inner + jnp.dot(sc, vh, preferred_element_type=_F32)
    st = state[...]
    cross = jnp.dot(qr.astype(_BF16), st.astype(_BF16), preferred_element_type=_F32) * xi[...]
    upd = lax.dot_general((kr * zeta[...]).astype(_BF16), rv.astype(_BF16), _TN_DIMS,
                          preferred_element_type=_F32)
    state[...] = cdec[...] * st + bdmask[...] * upd
    o = inner + cross
    msq = jnp.dot((o * o).astype(_BF16), gmat[...], preferred_element_type=_F32)

    lane_h = lax.broadcasted_iota(jnp.int32, (1, HEAD_W), 1)
    first_m = lane_h < (MLA_NOPE + MLA_ROPE // 2)
    cm, sm = cosm[rows, :], sinm[rows, :]
    gkv = gk[...]
    q_cos, q_sin = gq[...] * cm, gq_rot[...] * sm
    krg = kr128 * gkv
    k_rope = krg * cm + _rot_pairs(krg, MLA_ROPE // 2, first_m) * sm
    one_q = jnp.where(lane_h == MLA_QK, 1.0, 0.0)
    rowg = i * STEP + sub * TQ + lax.broadcasted_iota(jnp.int32, (TQ, 1), 0)
    kbias = jnp.where((rowg < PAD) & (lane_h == MLA_QK), MASK_VALUE, 0.0)
    for h in range(MLA_HEADS):
        sl = slice(h * HEAD_W, (h + 1) * HEAD_W)
        rsl = slice((h % 2) * HEAD_W, (h % 2 + 1) * HEAD_W)
        q_ref[0, rows, sl] = (rq2[h // 2][:, rsl] * (qraw[:, sl] * q_cos + qrot[:, sl] * q_sin)
                              + one_q).astype(_BF16)
        k_ref[0, rows, sl] = (rk2[h // 2][:, rsl] * (kraw[:, sl] * gkv + k_rope) + kbias).astype(_BF16)

    cb = proj_c[:, OFF_CB:OFF_CB + CONV_W]
    u = proj_c[:, OFF_CC:OFF_CC + CONV_W] * proj_c[:, OFF_CH:OFF_CH + CONV_W]
    ubuf[8:8 + TQ, :] = u
    cw = conv_w[...]
    conv = cw[0:1] * ubuf[6:6 + TQ, :] + cw[1:2] * ubuf[7:7 + TQ, :] + cw[2:3] * u
    ubuf[0:8, :] = u[TQ - 8:, :]
    ycr_ref[0, rows, 0:CONV_W] = (cb * conv).astype(_BF16)

    o_n = o * lax.rsqrt(msq + NORM_EPS) * ret_g[...]
    gate = rg * (1.0 / (1.0 + jnp.exp(-rg)))
    ycr_ref[0, rows, CONV_W:MIX_HALF] = (gate * o_n).astype(_BF16)


def _attn_kernel(q_ref, k_ref, vt_ref, ot_ref, acc_scr, m_scr, s_scr):
    causal = (lax.broadcasted_iota(jnp.int32, (TQ, 1), 0) <= lax.broadcasted_iota(jnp.int32, (1, TQ), 1))
    heads = [slice(h * HEAD_W, (h + 1) * HEAD_W) for h in range(MLA_HEADS)]
    vheads = [slice(h * VT_ROWS, (h + 1) * VT_ROWS) for h in range(MLA_HEADS)]

    def tile_start(t):
        return t * TQ if isinstance(t, int) else pl.multiple_of(t * TQ, TQ)

    def scores(i, j, h):
        qs, ks = tile_start(i), tile_start(j)
        return lax.dot_general(k_ref[0, pl.ds(ks, TQ), heads[h]], q_ref[0, pl.ds(qs, TQ), heads[h]], _NT_DIMS,
                               preferred_element_type=_F32)

    def stage(i, j, nxt, diagonal):
        for h in range(MLA_HEADS):
            s = s_scr[h]
            s_scr[h] = scores(nxt[0], nxt[1], h)
            if diagonal:
                s = jnp.where(causal, s, MASK_VALUE)
            m_prev = m_scr[h]
            m_new = jnp.maximum(m_prev, jnp.max(s, axis=0, keepdims=True))
            p = jnp.exp2(s - m_new)
            alpha = jnp.exp2(m_prev - m_new)
            pv = jnp.dot(vt_ref[0, j, vheads[h], :], p.astype(_BF16), preferred_element_type=_F32)
            acc_scr[h] = acc_scr[h] * alpha + pv
            m_scr[h] = m_new

    for t in range(FIRST_TILE):
        ot_ref[0, t] = jnp.zeros((O_W, TQ), _BF16)
    for h in range(MLA_HEADS):
        s_scr[h] = scores(FIRST_TILE, FIRST_TILE, h)

    def tile_body(i, carry):
        m_scr[...] = jnp.full(m_scr.shape, MASK_VALUE, _F32)
        acc_scr[...] = jnp.zeros(acc_scr.shape, _F32)
        n_full = i - FIRST_TILE

        def run(j0, n):
            for u in range(n):
                stage(i, j0 + u, (i, j0 + u + 1), False)

        def unrolled_body(jj, c):
            run(FIRST_TILE + jj * KV_UNROLL, KV_UNROLL)
            return c

        lax.fori_loop(0, n_full // KV_UNROLL, unrolled_body, 0)
        done = FIRST_TILE + (n_full // KV_UNROLL) * KV_UNROLL
        size = KV_UNROLL // 2
        while size >= 1:
            take = (n_full & size) != 0
            pl.when(take)(functools.partial(run, done, size))
            done = done + jnp.where(take, size, 0)
            size //= 2
        stage(i, i, (jnp.minimum(i + 1, NB - 1), FIRST_TILE), True)
        for h in range(MLA_HEADS):
            a = acc_scr[h]
            inv = 1.0 / a[MLA_V:MLA_V + 8]
            ot_ref[0, i, h * MLA_V:(h + 1) * MLA_V, :] = (
                a[0:MLA_V] * jnp.concatenate([inv] * (MLA_V // 8), axis=0)).astype(_BF16)
        return carry

    lax.fori_loop(FIRST_TILE, NB, tile_body, 0)


def _post_step(first_layer, last_layer, x_ref, meta_ref, ycr_ref, ot_ref, w_out, g_mlp, w1, w2, out_ref):
    if last_layer:
        x = x_ref[0]
    else:
        x = _load_step(first_layer, pl.program_id(1), x_ref, meta_ref)
    tiles = [slice(t * TQ, (t + 1) * TQ) for t in range(SUB)]
    x1 = [x[r] + jnp.dot(ycr_ref[0, r, :], w_out[0:MIX_HALF, :], preferred_element_type=_F32)
          + lax.dot_general(ot_ref[0, t], w_out[MIX_HALF:, :], _TN_DIMS, preferred_element_type=_F32)
          for t, r in enumerate(tiles)]
    up = [jnp.dot((_rms(v, D_MODEL) * g_mlp[...]).astype(_BF16), w1[...], preferred_element_type=_F32)
          for v in x1]
    down = [jnp.dot(jnp.square(jnp.maximum(u, 0.0)).astype(_BF16), w2[...], preferred_element_type=_F32)
            for u in up]
    for t, r in enumerate(tiles):
        out_ref[0, r, :] = x1[t] + down[t]


def _post_kernel(first_layer, last_layer, *refs):
    if first_layer:
        x_ref, meta_ref = refs[:2]
        refs = refs[2:]
    else:
        x_ref, meta_ref = refs[0], None
        refs = refs[1:]
    if last_layer:
        _post_step(first_layer, last_layer, x_ref, meta_ref, *refs)
        return
    ycr_ref, ot_ref, w_out, g_mlp, w1, w2, out_ref, h0 = refs
    b = pl.program_id(0)
    i = pl.program_id(1)
    fresh = (b == 0) | (i > 0)

    @pl.when(fresh)
    def _():
        _post_step(first_layer, last_layer, x_ref, meta_ref, ycr_ref, ot_ref, w_out, g_mlp, w1, w2, out_ref)

    @pl.when((b == 0) & (i == 0))
    def _():
        h0[...] = out_ref[0]

    @pl.when(jnp.logical_not(fresh))
    def _():
        out_ref[0] = h0[...]


def _const_spec(shape):
    nd = len(shape)
    return pl.BlockSpec(shape, lambda b, i, _nd=nd: (0,) * _nd, pipeline_mode=pl.Buffered(1))


def _params(n_axes=2):
    return pltpu.CompilerParams(dimension_semantics=("arbitrary",) * n_axes, vmem_limit_bytes=VMEM_LIMIT)


def _rope_partner(lh):
    half = MLA_ROPE // 2
    return jnp.where((lh >= MLA_NOPE) & (lh < MLA_NOPE + half), lh + half,
                     jnp.where((lh >= MLA_NOPE + half) & (lh < MLA_QK), lh - half, lh))


def _tables():
    pos = jnp.maximum(jnp.arange(TP, dtype=_F32) - PAD, 0.0)
    inv_r = 1.0 / (ROPE_BASE ** (jnp.arange(0, RET_DK, 2, dtype=_F32) / RET_DK))
    ang_r = pos[:, None] * inv_r[None, :]
    l = jnp.arange(RET_W)
    cosr = jnp.cos(ang_r)[:, l % (RET_DK // 2)]
    sinr = jnp.sin(ang_r)[:, l % (RET_DK // 2)] * jnp.where((l % RET_DK) < RET_DK // 2, -1.0, 1.0)[None, :]
    inv_m = 1.0 / (ROPE_BASE ** (jnp.arange(0, MLA_ROPE, 2, dtype=_F32) / MLA_ROPE))
    ang_m = pos[:, None] * inv_m[None, :]
    lh = jnp.arange(HEAD_W)
    in_rope = (lh >= MLA_NOPE) & (lh < MLA_QK)
    fidx = jnp.clip(lh - MLA_NOPE, 0, MLA_ROPE - 1) % (MLA_ROPE // 2)
    cosm = jnp.where(in_rope[None, :], jnp.cos(ang_m)[:, fidx], 1.0)
    sgn = jnp.where(lh < MLA_NOPE + MLA_ROPE // 2, -1.0, 1.0)
    sinm = jnp.where(in_rope[None, :], jnp.sin(ang_m)[:, fidx] * sgn[None, :], 0.0)
    log_g = jnp.log1p(-jnp.exp2(-5.0 - jnp.arange(RET_HEADS, dtype=_F32)))
    idx = jnp.arange(TQ, dtype=_F32)
    diff = idx[:, None] - idx[None, :]
    dmat = jnp.where(diff >= 0, jnp.exp(jnp.maximum(diff, 0.0)[None] * log_g[:, None, None]), 0.0)
    lg_lane = log_g[l // RET_DK]
    xi = jnp.exp((idx[:, None] + 1.0) * lg_lane[None, :])
    zeta = jnp.exp((TQ - 1.0 - idx[:, None]) * lg_lane[None, :])
    cdec = jnp.exp(TQ * lg_lane)[None, :]
    same = (l[:, None] // RET_DK) == (l[None, :] // RET_DK)
    bdmask = same.astype(_F32)
    gmat = (same.astype(_F32) / RET_DK).astype(_BF16)
    l2 = jnp.arange(2 * HEAD_W)
    gpair = (((l2[:, None] // HEAD_W) == (l2[None, :] // HEAD_W)).astype(_F32) / MLA_QK).astype(_BF16)
    return dict(cosr=cosr, sinr=sinr, cosm=cosm, sinm=sinm, dmat=dmat, xi=xi, zeta=zeta, cdec=cdec,
                bdmask=bdmask, gmat=gmat, gpair=gpair)


def _layer_weights(l, attn_norm_g, w_in, conv_w, ret_gn_g, q_norm_g, w_uq, kv_norm_g, w_ukv,
                   q_head_norm_g, k_head_norm_g, w_out, mlp_norm_g, w_mlp_in, w_mlp_out):
    wi = w_in[l]
    w_in_ext = jnp.concatenate(
        [wi[:, :OFF_KR], jnp.zeros((D_MODEL, MLA_NOPE), _F32), wi[:, OFF_KR:],
         jnp.zeros((D_MODEL, HEAD_W - MLA_QK), _F32)], axis=1).astype(_BF16)
    partner = _rope_partner(jnp.arange(HEAD_W))
    is_rope = partner != jnp.arange(HEAD_W)
    wq = jnp.pad(w_uq[l].reshape(Q_LORA, MLA_HEADS, MLA_QK), ((0, 0), (0, 0), (0, HEAD_W - MLA_QK)))
    wq_rot = jnp.where(is_rope[None, None, :], wq[:, :, partner], 0.0)
    wkv = w_ukv[l].reshape(KV_LORA, MLA_HEADS, MLA_NOPE + MLA_V)
    w_k = jnp.pad(wkv[:, :, :MLA_NOPE], ((0, 0), (0, 0), (0, HEAD_W - MLA_NOPE))).reshape(KV_LORA, ATT_W)
    w_vt = jnp.pad(wkv[:, :, MLA_NOPE:], ((0, 0), (0, 0), (0, VT_ROWS - MLA_V))).reshape(KV_LORA, VT_W).T
    gq128 = jnp.pad(q_head_norm_g[l], (0, HEAD_W - MLA_QK)) * EXP2_SCALE
    gk128 = jnp.pad(k_head_norm_g[l], (0, HEAD_W - MLA_QK))
    return dict(
        g_attn=attn_norm_g[l][None, :], w_in=w_in_ext, conv_w=conv_w[l], ret_g=ret_gn_g[l].reshape(1, RET_W),
        qn_g=q_norm_g[l][None, :], w_uq=wq.reshape(Q_LORA, ATT_W).astype(_BF16),
        w_uq_rot=wq_rot.reshape(Q_LORA, ATT_W).astype(_BF16), kvn_g=kv_norm_g[l][None, :],
        w_k=w_k.astype(_BF16), w_vt=w_vt.astype(_BF16), gq=gq128[None, :],
        gq_rot=jnp.where(is_rope, gq128[partner], 0.0)[None, :], gk=gk128[None, :],
        w_out=w_out[l].astype(_BF16), g_mlp=mlp_norm_g[l][None, :], w1=w_mlp_in[l].astype(_BF16),
        w2=w_mlp_out[l].astype(_BF16))


def _x_specs(first_layer):
    if first_layer:
        return [pl.BlockSpec((1, STEP, D_MODEL), lambda b, i: (b, jnp.maximum(i - 1, 0), 0)),
                _const_spec((STEP, D_MODEL))]
    return [pl.BlockSpec((1, STEP, D_MODEL), lambda b, i: (b, i, 0))]


_PRE_CONSTS = [("g_attn", "w"), ("w_in", "w"), ("conv_w", "w"), ("cosr", "row"), ("sinr", "row"), ("dmat", "t"),
               ("xi", "t"), ("zeta", "t"), ("cdec", "t"), ("bdmask", "t"), ("gmat", "t"), ("ret_g", "w"),
               ("qn_g", "w"), ("w_uq", "w"), ("w_uq_rot", "w"), ("kvn_g", "w"), ("w_k", "w"), ("w_vt", "w"),
               ("gpair", "t"), ("gq", "w"), ("gq_rot", "w"), ("gk", "w"), ("cosm", "row"), ("sinm", "row")]


def _pre_call(first_layer, batch, x_args, w, t):
    step = lambda width: pl.BlockSpec((1, STEP, width), lambda b, i: (b, i, 0))
    in_specs = _x_specs(first_layer)
    args = list(x_args)
    for name, kind in _PRE_CONSTS:
        arr = w[name] if kind == "w" else t[name]
        if kind == "row":
            in_specs.append(pl.BlockSpec((STEP, arr.shape[1]), lambda b, i: (i, 0)))
        else:
            in_specs.append(_const_spec(arr.shape))
        args.append(arr)
    out_shape = [jax.ShapeDtypeStruct((batch, TP, MIX_HALF), _BF16),
                 jax.ShapeDtypeStruct((batch, TP, ATT_W), _BF16),
                 jax.ShapeDtypeStruct((batch, TP, ATT_W), _BF16),
                 jax.ShapeDtypeStruct((batch, NB, VT_W, TQ), _BF16)]
    scratch = [pltpu.VMEM((TQ + 8, CONV_W), _F32), pltpu.VMEM((RET_W, RET_W), _F32),
               pltpu.VMEM((STEP, MIX_HALF), _BF16), pltpu.VMEM((STEP, ATT_W), _BF16),
               pltpu.VMEM((STEP, ATT_W), _BF16), pltpu.VMEM((SUB, VT_W, TQ), _BF16),
               pltpu.VMEM((8, CONV_W), _F32), pltpu.VMEM((RET_W, RET_W), _F32)]
    return pl.pallas_call(
        functools.partial(_pre_kernel, first_layer),
        grid=(batch, NS), in_specs=in_specs,
        out_specs=[step(MIX_HALF), step(ATT_W), step(ATT_W),
                   pl.BlockSpec((1, SUB, VT_W, TQ), lambda b, i: (b, i, 0, 0))],
        out_shape=out_shape, scratch_shapes=scratch, compiler_params=_params(), name="pre")(*args)


def _attn_call(batch, q, k, vt):
    return pl.pallas_call(
        _attn_kernel, grid=(batch,),
        in_specs=[pl.BlockSpec((1, TP, ATT_W), lambda b: (b, 0, 0)),
                  pl.BlockSpec((1, TP, ATT_W), lambda b: (b, 0, 0)),
                  pl.BlockSpec((1, NB, VT_W, TQ), lambda b: (b, 0, 0, 0))],
        out_specs=pl.BlockSpec((1, NB, O_W, TQ), lambda b: (b, 0, 0, 0)),
        out_shape=jax.ShapeDtypeStruct((batch, NB, O_W, TQ), _BF16),
        scratch_shapes=[pltpu.VMEM((MLA_HEADS, VT_ROWS, TQ), _F32), pltpu.VMEM((MLA_HEADS, 1, TQ), _F32),
                        pltpu.VMEM((MLA_HEADS, TQ, TQ), _F32)],
        compiler_params=_params(1), name="attn")(q, k, vt)


def _post_call(first_layer, last_layer, batch, x_args, ycr, ot, w):
    off = 1 if last_layer else 0
    if last_layer:
        x_specs = [pl.BlockSpec((1, STEP, D_MODEL), lambda b, i: (b, i + off, 0))]
        out_shape = jax.ShapeDtypeStruct((batch, SEQ, D_MODEL), _F32)
        n_steps = NS - 1
        scratch = []
    else:
        x_specs = _x_specs(first_layer)
        out_shape = jax.ShapeDtypeStruct((batch, TP, D_MODEL), _F32)
        n_steps = NS
        scratch = [pltpu.VMEM((STEP, D_MODEL), _F32)]
    const_names = ["w_out", "g_mlp", "w1", "w2"]
    return pl.pallas_call(
        functools.partial(_post_kernel, first_layer, last_layer),
        grid=(batch, n_steps),
        in_specs=x_specs + [pl.BlockSpec((1, STEP, MIX_HALF), lambda b, i: (b, i + off, 0)),
                            pl.BlockSpec((1, SUB, O_W, TQ), lambda b, i: (b, i + off, 0, 0))]
        + [_const_spec(w[n].shape) for n in const_names],
        out_specs=pl.BlockSpec((1, STEP, D_MODEL), lambda b, i: (b, i, 0)),
        out_shape=out_shape, scratch_shapes=scratch, compiler_params=_params(), name="post")(
            *x_args, ycr, ot, *[w[n] for n in const_names])


def kernel(x, meta_tokens, attn_norm_g, w_in, conv_w, ret_gn_g, q_norm_g, w_uq, kv_norm_g, w_ukv,
           q_head_norm_g, k_head_norm_g, w_out, mlp_norm_g, w_mlp_in, w_mlp_out):
    batch = x.shape[0]
    depth = w_in.shape[0]
    assert x.shape[1:] == (SEQ, D_MODEL) and depth >= 2
    t = _tables()
    meta_pad = jnp.concatenate([jnp.zeros((PAD, D_MODEL), x.dtype), meta_tokens.astype(x.dtype)], axis=0)
    h = None
    for l in range(depth):
        w = _layer_weights(l, attn_norm_g, w_in, conv_w, ret_gn_g, q_norm_g, w_uq, kv_norm_g, w_ukv,
                           q_head_norm_g, k_head_norm_g, w_out, mlp_norm_g, w_mlp_in, w_mlp_out)
        first, last = l == 0, l == depth - 1
        x_args = (x, meta_pad) if first else (h,)
        ycr, q, k, vt = _pre_call(first, batch, x_args, w, t)
        ot = _attn_call(batch, q, k, vt)
        h = _post_call(first, last, batch, x_args, ycr, ot, w)
    return h
```

```python
import functools
import math

import jax
import jax.numpy as jnp
from jax import lax
from jax.experimental import pallas as pl
from jax.experimental.pallas import tpu as pltpu

D_MODEL = 1024
SEQ = 2048
N_META = 16
CONV_W = 256
CONV_K = 3
RET_HEADS = 4
RET_DK = 64
RET_W = RET_HEADS * RET_DK
MLA_HEADS = 8
MLA_NOPE = 64
MLA_ROPE = 32
MLA_V = 64
MLA_QK = MLA_NOPE + MLA_ROPE
Q_LORA = 256
KV_LORA = 128
D_FF = 4 * D_MODEL
ROPE_BASE = 10000.0
NORM_EPS = 1e-6

LANES = 128
BF16_ROWS = 16
HEAD_W = LANES
ATT_W = MLA_HEADS * HEAD_W
VT_ROWS = MLA_V + BF16_ROWS
VT_W = MLA_HEADS * VT_ROWS
MIX_HALF = CONV_W + RET_W
O_W = MLA_HEADS * MLA_V

TQ = 256
SUB = 2
STEP = SUB * TQ
PAD = STEP - N_META
TP = SEQ + STEP
NB = TP // TQ
NS = TP // STEP
FIRST_TILE = PAD // TQ

OFF_CB, OFF_CC, OFF_CH = 0, 256, 512
OFF_RQ, OFF_RK, OFF_RV, OFF_RG = 768, 1024, 1280, 1536
OFF_CQ, OFF_CKV, OFF_KR = 1792, 2048, 2176
D_IN_EXT = OFF_KR + HEAD_W

KV_UNROLL = 4
MASK_VALUE = -1e30
EXP2_SCALE = math.log2(math.e) / math.sqrt(MLA_QK)
VMEM_LIMIT = 56 * 1024 * 1024

_F32 = jnp.float32
_BF16 = jnp.bfloat16
_NT_DIMS = (((1,), (1,)), ((), ()))
_TN_DIMS = (((0,), (0,)), ((), ()))


def _rms(x, n):
    return x * lax.rsqrt(jnp.sum(x * x, axis=-1, keepdims=True) * (1.0 / n) + NORM_EPS)


def _rot_pairs(z, half, first):
    w = z.shape[-1]
    return jnp.where(first, pltpu.roll(z, w - half, axis=1), pltpu.roll(z, half, axis=1))


def _load_step(first_layer, i, x_ref, meta_ref):
    x = x_ref[0]
    if first_layer:
        return jnp.where(i == 0, meta_ref[...], x)
    row = lax.broadcasted_iota(jnp.int32, (STEP, 1), 0)
    return jnp.where((i > 0) | (row >= PAD), x, 0.0)


def _pre_kernel(first_layer, *refs):
    if first_layer:
        x_ref, meta_ref = refs[:2]
        refs = refs[2:]
    else:
        x_ref, meta_ref = refs[0], None
        refs = refs[1:]
    consts = refs[:24]
    ycr_ref, q_ref, k_ref, vt_ref, ubuf, state, ycr0, q0, k0, vt0, halo0, state0 = refs[24:]
    b = pl.program_id(0)
    i = pl.program_id(1)
    fresh = (b == 0) | (i > 0)

    @pl.when((b == 0) & (i == 0))
    def _():
        ubuf[0:8, :] = jnp.zeros((8, CONV_W), _F32)
        state[...] = jnp.zeros_like(state)

    @pl.when(fresh)
    def _():
        x = _load_step(first_layer, i, x_ref, meta_ref)
        for sub in range(SUB):
            _pre_tile(i, sub, x[sub * TQ:(sub + 1) * TQ], consts, ycr_ref, q_ref, k_ref, vt_ref, ubuf, state)

    @pl.when((b == 0) & (i == 0))
    def _():
        ycr0[...] = ycr_ref[0]
        q0[...] = q_ref[0]
        k0[...] = k_ref[0]
        vt0[...] = vt_ref[0]
        halo0[...] = ubuf[0:8, :]
        state0[...] = state[...]

    @pl.when(jnp.logical_not(fresh))
    def _():
        ycr_ref[0] = ycr0[...]
        q_ref[0] = q0[...]
        k_ref[0] = k0[...]
        vt_ref[0] = vt0[...]
        ubuf[0:8, :] = halo0[...]
        state[...] = state0[...]


def _pre_tile(i, sub, x, consts, ycr_ref, q_ref, k_ref, vt_ref, ubuf, state):
    (g_attn, w_in, conv_w, cosr, sinr, dmat, xi, zeta, cdec, bdmask, gmat, ret_g,
     qn_g, w_uq, w_uq_rot, kvn_g, w_k, w_vt, gpair, gq, gq_rot, gk, cosm, sinm) = consts
    rows = slice(sub * TQ, (sub + 1) * TQ)
    hn = (_rms(x, D_MODEL) * g_attn[...]).astype(_BF16)
    proj_m = jnp.dot(hn, w_in[:, OFF_CQ:D_IN_EXT], preferred_element_type=_F32)
    proj_r = jnp.dot(hn, w_in[:, OFF_RQ:OFF_CQ], preferred_element_type=_F32)

    cq = proj_m[:, 0:Q_LORA]
    ckv = proj_m[:, Q_LORA:Q_LORA + KV_LORA]
    kr128 = proj_m[:, Q_LORA + KV_LORA:]
    cqn = (_rms(cq, Q_LORA) * qn_g[...]).astype(_BF16)
    ckvn = (_rms(ckv, KV_LORA) * kvn_g[...]).astype(_BF16)
    qraw = jnp.dot(cqn, w_uq[...], preferred_element_type=_F32)
    qrot = jnp.dot(cqn, w_uq_rot[...], preferred_element_type=_F32)
    kraw = jnp.dot(ckvn, w_k[...], preferred_element_type=_F32)
    vt = lax.dot_general(w_vt[...], ckvn, _NT_DIMS, preferred_element_type=_F32)
    vrow = lax.broadcasted_iota(jnp.int32, (VT_W, 1), 0)
    vt_ref[0, sub] = jnp.where((vrow % VT_ROWS) >= MLA_V, 1.0, vt).astype(_BF16)

    lane = lax.broadcasted_iota(jnp.int32, (1, RET_W), 1)
    first = (lane % RET_DK) < (RET_DK // 2)
    rq = proj_r[:, 0:RET_W]
    rk = proj_r[:, RET_W:2 * RET_W]
    rv = proj_r[:, 2 * RET_W:3 * RET_W]
    rg = proj_r[:, 3 * RET_W:]
    cr, sr = cosr[rows, :], sinr[rows, :]
    qr = rq * cr + _rot_pairs(rq, RET_DK // 2, first) * sr
    kr = (rk * cr + _rot_pairs(rk, RET_DK // 2, first) * sr) * (RET_DK ** -0.5)
    kb = kr.astype(_BF16)
    hmasks = [(lane // RET_DK) == h for h in range(RET_HEADS)]
    scs = [lax.dot_general(jnp.where(hmasks[h], qr, 0.0).astype(_BF16), kb, _NT_DIMS,
                           preferred_element_type=_F32) for h in range(RET_HEADS)]

    g2 = gpair[...]
    q2 = (qraw * qraw).astype(_BF16)
    k2 = (kraw * kraw).astype(_BF16)
    kr_ms = jnp.dot((kr128 * kr128).astype(_BF16), g2[0:HEAD_W, 0:HEAD_W], preferred_element_type=_F32)
    kr_ms2 = jnp.concatenate([kr_ms, kr_ms], axis=1)
    rq2, rk2 = [], []
    for pr in range(MLA_HEADS // 2):
        psl = slice(2 * pr * HEAD_W, (2 * pr + 2) * HEAD_W)
        rq2.append(lax.rsqrt(jnp.dot(q2[:, psl], g2, preferred_element_type=_F32) + NORM_EPS))
        rk2.append(lax.rsqrt(jnp.dot(k2[:, psl], g2, preferred_element_type=_F32) + kr_ms2 + NORM_EPS))

    proj_c = jnp.dot(hn, w_in[:, OFF_CB:OFF_RQ], preferred_element_type=_F32)

    st = state[...]
    cross = jnp.dot(qr.astype(_BF16), st.astype(_BF16), preferred_element_type=_F32) * xi[...]
    upd = lax.dot_general((kr * zeta[...]).astype(_BF16), rv.astype(_BF16), _TN_DIMS,
                          preferred_element_type=_F32)
    state[...] = cdec[...] * st + bdmask[...] * upd
    inner = jnp.zeros((TQ, RET_W), _F32)
    for h in range(RET_HEADS):
        sc = (scs[h] * dmat[h]).astype(_BF16)
        vh = jnp.where(hmasks[h], rv, 0.0).astype(_BF16)
        inner = inner + jnp.dot(sc, vh, preferred_element_type=_F32)
    o = inner + cross
    msq = jnp.dot((o * o).astype(_BF16), gmat[...], preferred_element_type=_F32)

    lane_h = lax.broadcasted_iota(jnp.int32, (1, HEAD_W), 1)
    first_m = lane_h < (MLA_NOPE + MLA_ROPE // 2)
    cm, sm = cosm[rows, :], sinm[rows, :]
    gkv = gk[...]
    q_cos, q_sin = gq[...] * cm, gq_rot[...] * sm
    krg = kr128 * gkv
    k_rope = krg * cm + _rot_pairs(krg, MLA_ROPE // 2, first_m) * sm
    one_q = jnp.where(lane_h == MLA_QK, 1.0, 0.0)
    rowg = i * STEP + sub * TQ + lax.broadcasted_iota(jnp.int32, (TQ, 1), 0)
    kbias = jnp.where((rowg < PAD) & (lane_h == MLA_QK), MASK_VALUE, 0.0)
    for h in range(MLA_HEADS):
        sl = slice(h * HEAD_W, (h + 1) * HEAD_W)
        rsl = slice((h % 2) * HEAD_W, (h % 2 + 1) * HEAD_W)
        q_ref[0, rows, sl] = (rq2[h // 2][:, rsl] * (qraw[:, sl] * q_cos + qrot[:, sl] * q_sin)
                              + one_q).astype(_BF16)
        k_ref[0, rows, sl] = (rk2[h // 2][:, rsl] * (kraw[:, sl] * gkv + k_rope) + kbias).astype(_BF16)

    cb = proj_c[:, OFF_CB:OFF_CB + CONV_W]
    u = proj_c[:, OFF_CC:OFF_CC + CONV_W] * proj_c[:, OFF_CH:OFF_CH + CONV_W]
    ubuf[8:8 + TQ, :] = u
    cw = conv_w[...]
    conv = cw[0:1] * ubuf[6:6 + TQ, :] + cw[1:2] * ubuf[7:7 + TQ, :] + cw[2:3] * u
    ubuf[0:8, :] = u[TQ - 8:, :]
    ycr_ref[0, rows, 0:CONV_W] = (cb * conv).astype(_BF16)

    o_n = o * lax.rsqrt(msq + NORM_EPS) * ret_g[...]
    gate = rg * (1.0 / (1.0 + jnp.exp(-rg)))
    ycr_ref[0, rows, CONV_W:MIX_HALF] = (gate * o_n).astype(_BF16)


def _attn_kernel(q_first, q_ref, k_ref, vt_ref, ot_ref, acc_scr, m_scr, s_scr):
    causal = (lax.broadcasted_iota(jnp.int32, (TQ, 1), 0) <= lax.broadcasted_iota(jnp.int32, (1, TQ), 1))
    heads = [slice(h * HEAD_W, (h + 1) * HEAD_W) for h in range(MLA_HEADS)]
    vheads = [slice(h * VT_ROWS, (h + 1) * VT_ROWS) for h in range(MLA_HEADS)]

    def tile_start(t):
        return t * TQ if isinstance(t, int) else pl.multiple_of(t * TQ, TQ)

    def scores(i, j, h):
        qs, ks = tile_start(i), tile_start(j)
        return lax.dot_general(k_ref[0, pl.ds(ks, TQ), heads[h]], q_ref[0, pl.ds(qs, TQ), heads[h]], _NT_DIMS,
                               preferred_element_type=_F32)

    def stage(i, j, nxt, diagonal):
        for h in range(MLA_HEADS):
            s = s_scr[h]
            s_scr[h] = scores(nxt[0], nxt[1], h)
            if diagonal:
                s = jnp.where(causal, s, MASK_VALUE)
            m_prev = m_scr[h]
            m_new = jnp.maximum(m_prev, jnp.max(s, axis=0, keepdims=True))
            p = jnp.exp2(s - m_new)
            alpha = jnp.exp2(m_prev - m_new)
            pv = jnp.dot(vt_ref[0, j, vheads[h], :], p.astype(_BF16), preferred_element_type=_F32)
            acc_scr[h] = acc_scr[h] * alpha + pv
            m_scr[h] = m_new

    for t in range(q_first):
        ot_ref[0, t] = jnp.zeros((O_W, TQ), _BF16)
    for h in range(MLA_HEADS):
        s_scr[h] = scores(q_first, FIRST_TILE, h)

    def tile_body(i, carry):
        m_scr[...] = jnp.full(m_scr.shape, MASK_VALUE, _F32)
        acc_scr[...] = jnp.zeros(acc_scr.shape, _F32)
        n_full = i - FIRST_TILE

        def run(j0, n):
            for u in range(n):
                stage(i, j0 + u, (i, j0 + u + 1), False)

        def unrolled_body(jj, c):
            run(FIRST_TILE + jj * KV_UNROLL, KV_UNROLL)
            return c

        def finish(rem):
            run(i - rem, rem)
            stage(i, i, (jnp.minimum(i + 1, NB - 1), FIRST_TILE), True)
            for h in range(MLA_HEADS):
                a = acc_scr[h]
                inv = 1.0 / a[MLA_V:MLA_V + 8]
                ot_ref[0, i, h * MLA_V:(h + 1) * MLA_V, :] = (
                    a[0:MLA_V] * jnp.concatenate([inv] * (MLA_V // 8), axis=0)).astype(_BF16)

        lax.fori_loop(0, n_full // KV_UNROLL, unrolled_body, 0)
        for rem in range(KV_UNROLL):
            pl.when(n_full % KV_UNROLL == rem)(functools.partial(finish, rem))
        return carry

    lax.fori_loop(q_first, NB, tile_body, 0)


def _post_step(first_layer, last_layer, x_ref, meta_ref, ycr_ref, ot_ref, w_out, g_mlp, w1, w2, out_ref):
    if last_layer:
        x = x_ref[0]
    else:
        x = _load_step(first_layer, pl.program_id(1), x_ref, meta_ref)
    tiles = [slice(t * TQ, (t + 1) * TQ) for t in range(SUB)]
    x1 = [x[r] + jnp.dot(ycr_ref[0, r, :], w_out[0:MIX_HALF, :], preferred_element_type=_F32)
          + lax.dot_general(ot_ref[0, t], w_out[MIX_HALF:, :], _TN_DIMS, preferred_element_type=_F32)
          for t, r in enumerate(tiles)]
    up = [jnp.dot((_rms(v, D_MODEL) * g_mlp[...]).astype(_BF16), w1[...], preferred_element_type=_F32)
          for v in x1]
    down = [jnp.dot(jnp.square(jnp.maximum(u, 0.0)).astype(_BF16), w2[...], preferred_element_type=_F32)
            for u in up]
    for t, r in enumerate(tiles):
        out_ref[0, r, :] = x1[t] + down[t]


def _post_kernel(first_layer, last_layer, *refs):
    if first_layer:
        x_ref, meta_ref = refs[:2]
        refs = refs[2:]
    else:
        x_ref, meta_ref = refs[0], None
        refs = refs[1:]
    if last_layer:
        _post_step(first_layer, last_layer, x_ref, meta_ref, *refs)
        return
    ycr_ref, ot_ref, w_out, g_mlp, w1, w2, out_ref, h0 = refs
    b = pl.program_id(0)
    i = pl.program_id(1)
    fresh = (b == 0) | (i > 0)

    @pl.when(fresh)
    def _():
        _post_step(first_layer, last_layer, x_ref, meta_ref, ycr_ref, ot_ref, w_out, g_mlp, w1, w2, out_ref)

    @pl.when((b == 0) & (i == 0))
    def _():
        h0[...] = out_ref[0]

    @pl.when(jnp.logical_not(fresh))
    def _():
        out_ref[0] = h0[...]


def _const_spec(shape):
    nd = len(shape)
    return pl.BlockSpec(shape, lambda b, i, _nd=nd: (0,) * _nd, pipeline_mode=pl.Buffered(1))


def _params(n_axes=2):
    return pltpu.CompilerParams(dimension_semantics=("arbitrary",) * n_axes, vmem_limit_bytes=VMEM_LIMIT)


def _rope_partner(lh):
    half = MLA_ROPE // 2
    return jnp.where((lh >= MLA_NOPE) & (lh < MLA_NOPE + half), lh + half,
                     jnp.where((lh >= MLA_NOPE + half) & (lh < MLA_QK), lh - half, lh))


def _tables():
    pos = jnp.maximum(jnp.arange(TP, dtype=_F32) - PAD, 0.0)
    inv_r = 1.0 / (ROPE_BASE ** (jnp.arange(0, RET_DK, 2, dtype=_F32) / RET_DK))
    ang_r = pos[:, None] * inv_r[None, :]
    l = jnp.arange(RET_W)
    cosr = jnp.cos(ang_r)[:, l % (RET_DK // 2)]
    sinr = jnp.sin(ang_r)[:, l % (RET_DK // 2)] * jnp.where((l % RET_DK) < RET_DK // 2, -1.0, 1.0)[None, :]
    inv_m = 1.0 / (ROPE_BASE ** (jnp.arange(0, MLA_ROPE, 2, dtype=_F32) / MLA_ROPE))
    ang_m = pos[:, None] * inv_m[None, :]
    lh = jnp.arange(HEAD_W)
    in_rope = (lh >= MLA_NOPE) & (lh < MLA_QK)
    fidx = jnp.clip(lh - MLA_NOPE, 0, MLA_ROPE - 1) % (MLA_ROPE // 2)
    cosm = jnp.where(in_rope[None, :], jnp.cos(ang_m)[:, fidx], 1.0)
    sgn = jnp.where(lh < MLA_NOPE + MLA_ROPE // 2, -1.0, 1.0)
    sinm = jnp.where(in_rope[None, :], jnp.sin(ang_m)[:, fidx] * sgn[None, :], 0.0)
    log_g = jnp.log1p(-jnp.exp2(-5.0 - jnp.arange(RET_HEADS, dtype=_F32)))
    idx = jnp.arange(TQ, dtype=_F32)
    diff = idx[:, None] - idx[None, :]
    dmat = jnp.where(diff >= 0, jnp.exp(jnp.maximum(diff, 0.0)[None] * log_g[:, None, None]), 0.0)
    lg_lane = log_g[l // RET_DK]
    xi = jnp.exp((idx[:, None] + 1.0) * lg_lane[None, :])
    zeta = jnp.exp((TQ - 1.0 - idx[:, None]) * lg_lane[None, :])
    cdec = jnp.exp(TQ * lg_lane)[None, :]
    same = (l[:, None] // RET_DK) == (l[None, :] // RET_DK)
    bdmask = same.astype(_F32)
    gmat = (same.astype(_F32) / RET_DK).astype(_BF16)
    l2 = jnp.arange(2 * HEAD_W)
    gpair = (((l2[:, None] // HEAD_W) == (l2[None, :] // HEAD_W)).astype(_F32) / MLA_QK).astype(_BF16)
    return dict(cosr=cosr, sinr=sinr, cosm=cosm, sinm=sinm, dmat=dmat, xi=xi, zeta=zeta, cdec=cdec,
                bdmask=bdmask, gmat=gmat, gpair=gpair)


def _layer_weights(l, attn_norm_g, w_in, conv_w, ret_gn_g, q_norm_g, w_uq, kv_norm_g, w_ukv,
                   q_head_norm_g, k_head_norm_g, w_out, mlp_norm_g, w_mlp_in, w_mlp_out):
    wi = w_in[l]
    w_in_ext = jnp.concatenate(
        [wi[:, :OFF_KR], jnp.zeros((D_MODEL, MLA_NOPE), _F32), wi[:, OFF_KR:],
         jnp.zeros((D_MODEL, HEAD_W - MLA_QK), _F32)], axis=1).astype(_BF16)
    partner = _rope_partner(jnp.arange(HEAD_W))
    is_rope = partner != jnp.arange(HEAD_W)
    wq = jnp.pad(w_uq[l].reshape(Q_LORA, MLA_HEADS, MLA_QK), ((0, 0), (0, 0), (0, HEAD_W - MLA_QK)))
    wq_rot = jnp.where(is_rope[None, None, :], wq[:, :, partner], 0.0)
    wkv = w_ukv[l].reshape(KV_LORA, MLA_HEADS, MLA_NOPE + MLA_V)
    w_k = jnp.pad(wkv[:, :, :MLA_NOPE], ((0, 0), (0, 0), (0, HEAD_W - MLA_NOPE))).reshape(KV_LORA, ATT_W)
    w_vt = jnp.pad(wkv[:, :, MLA_NOPE:], ((0, 0), (0, 0), (0, VT_ROWS - MLA_V))).reshape(KV_LORA, VT_W).T
    gq128 = jnp.pad(q_head_norm_g[l], (0, HEAD_W - MLA_QK)) * EXP2_SCALE
    gk128 = jnp.pad(k_head_norm_g[l], (0, HEAD_W - MLA_QK))
    return dict(
        g_attn=attn_norm_g[l][None, :], w_in=w_in_ext, conv_w=conv_w[l], ret_g=ret_gn_g[l].reshape(1, RET_W),
        qn_g=q_norm_g[l][None, :], w_uq=wq.reshape(Q_LORA, ATT_W).astype(_BF16),
        w_uq_rot=wq_rot.reshape(Q_LORA, ATT_W).astype(_BF16), kvn_g=kv_norm_g[l][None, :],
        w_k=w_k.astype(_BF16), w_vt=w_vt.astype(_BF16), gq=gq128[None, :],
        gq_rot=jnp.where(is_rope, gq128[partner], 0.0)[None, :], gk=gk128[None, :],
        w_out=w_out[l].astype(_BF16), g_mlp=mlp_norm_g[l][None, :], w1=w_mlp_in[l].astype(_BF16),
        w2=w_mlp_out[l].astype(_BF16))


def _x_specs(first_layer):
    if first_layer:
        return [pl.BlockSpec((1, STEP, D_MODEL), lambda b, i: (b, jnp.maximum(i - 1, 0), 0)),
                _const_spec((STEP, D_MODEL))]
    return [pl.BlockSpec((1, STEP, D_MODEL), lambda b, i: (b, i, 0))]


_PRE_CONSTS = [("g_attn", "w"), ("w_in", "w"), ("conv_w", "w"), ("cosr", "row"), ("sinr", "row"), ("dmat", "t"),
               ("xi", "t"), ("zeta", "t"), ("cdec", "t"), ("bdmask", "t"), ("gmat", "t"), ("ret_g", "w"),
               ("qn_g", "w"), ("w_uq", "w"), ("w_uq_rot", "w"), ("kvn_g", "w"), ("w_k", "w"), ("w_vt", "w"),
               ("gpair", "t"), ("gq", "w"), ("gq_rot", "w"), ("gk", "w"), ("cosm", "row"), ("sinm", "row")]


def _pre_call(first_layer, batch, x_args, w, t):
    step = lambda width: pl.BlockSpec((1, STEP, width), lambda b, i: (b, i, 0))
    in_specs = _x_specs(first_layer)
    args = list(x_args)
    for name, kind in _PRE_CONSTS:
        arr = w[name] if kind == "w" else t[name]
        if kind == "row":
            in_specs.append(pl.BlockSpec((STEP, arr.shape[1]), lambda b, i: (i, 0)))
        else:
            in_specs.append(_const_spec(arr.shape))
        args.append(arr)
    out_shape = [jax.ShapeDtypeStruct((batch, TP, MIX_HALF), _BF16),
                 jax.ShapeDtypeStruct((batch, TP, ATT_W), _BF16),
                 jax.ShapeDtypeStruct((batch, TP, ATT_W), _BF16),
                 jax.ShapeDtypeStruct((batch, NB, VT_W, TQ), _BF16)]
    scratch = [pltpu.VMEM((TQ + 8, CONV_W), _F32), pltpu.VMEM((RET_W, RET_W), _F32),
               pltpu.VMEM((STEP, MIX_HALF), _BF16), pltpu.VMEM((STEP, ATT_W), _BF16),
               pltpu.VMEM((STEP, ATT_W), _BF16), pltpu.VMEM((SUB, VT_W, TQ), _BF16),
               pltpu.VMEM((8, CONV_W), _F32), pltpu.VMEM((RET_W, RET_W), _F32)]
    return pl.pallas_call(
        functools.partial(_pre_kernel, first_layer),
        grid=(batch, NS), in_specs=in_specs,
        out_specs=[step(MIX_HALF), step(ATT_W), step(ATT_W),
                   pl.BlockSpec((1, SUB, VT_W, TQ), lambda b, i: (b, i, 0, 0))],
        out_shape=out_shape, scratch_shapes=scratch, compiler_params=_params(), name="pre")(*args)


def _attn_call(last_layer, batch, q, k, vt):
    q_first = FIRST_TILE + 1 if last_layer else FIRST_TILE
    return pl.pallas_call(
        functools.partial(_attn_kernel, q_first), grid=(batch,),
        in_specs=[pl.BlockSpec((1, TP, ATT_W), lambda b: (b, 0, 0)),
                  pl.BlockSpec((1, TP, ATT_W), lambda b: (b, 0, 0)),
                  pl.BlockSpec((1, NB, VT_W, TQ), lambda b: (b, 0, 0, 0))],
        out_specs=pl.BlockSpec((1, NB, O_W, TQ), lambda b: (b, 0, 0, 0)),
        out_shape=jax.ShapeDtypeStruct((batch, NB, O_W, TQ), _BF16),
        scratch_shapes=[pltpu.VMEM((MLA_HEADS, VT_ROWS, TQ), _F32), pltpu.VMEM((MLA_HEADS, 1, TQ), _F32),
                        pltpu.VMEM((MLA_HEADS, TQ, TQ), _F32)],
        compiler_params=_params(1), name="attn")(q, k, vt)


def _post_call(first_layer, last_layer, batch, x_args, ycr, ot, w):
    off = 1 if last_layer else 0
    if last_layer:
        x_specs = [pl.BlockSpec((1, STEP, D_MODEL), lambda b, i: (b, i + off, 0))]
        out_shape = jax.ShapeDtypeStruct((batch, SEQ, D_MODEL), _F32)
        n_steps = NS - 1
        scratch = []
    else:
        x_specs = _x_specs(first_layer)
        out_shape = jax.ShapeDtypeStruct((batch, TP, D_MODEL), _F32)
        n_steps = NS
        scratch = [pltpu.VMEM((STEP, D_MODEL), _F32)]
    const_names = ["w_out", "g_mlp", "w1", "w2"]
    return pl.pallas_call(
        functools.partial(_post_kernel, first_layer, last_layer),
        grid=(batch, n_steps),
        in_specs=x_specs + [pl.BlockSpec((1, STEP, MIX_HALF), lambda b, i: (b, i + off, 0)),
                            pl.BlockSpec((1, SUB, O_W, TQ), lambda b, i: (b, i + off, 0, 0))]
        + [_const_spec(w[n].shape) for n in const_names],
        out_specs=pl.BlockSpec((1, STEP, D_MODEL), lambda b, i: (b, i, 0)),
        out_shape=out_shape, scratch_shapes=scratch, compiler_params=_params(), name="post")(
            *x_args, ycr, ot, *[w[n] for n in const_names])


def kernel(x, meta_tokens, attn_norm_g, w_in, conv_w, ret_gn_g, q_norm_g, w_uq, kv_norm_g, w_ukv,
           q_head_norm_g, k_head_norm_g, w_out, mlp_norm_g, w_mlp_in, w_mlp_out):
    batch = x.shape[0]
    depth = w_in.shape[0]
    assert x.shape[1:] == (SEQ, D_MODEL) and depth >= 2
    t = _tables()
    meta_pad = jnp.concatenate([jnp.zeros((PAD, D_MODEL), x.dtype), meta_tokens.astype(x.dtype)], axis=0)
    h = None
    for l in range(depth):
        w = _layer_weights(l, attn_norm_g, w_in, conv_w, ret_gn_g, q_norm_g, w_uq, kv_norm_g, w_ukv,
                           q_head_norm_g, k_head_norm_g, w_out, mlp_norm_g, w_mlp_in, w_mlp_out)
        first, last = l == 0, l == depth - 1
        x_args = (x, meta_pad) if first else (h,)
        ycr, q, k, vt = _pre_call(first, batch, x_args, w, t)
        ot = _attn_call(last, batch, q, k, vt)
        h = _post_call(first, last, batch, x_args, ycr, ot, w)
    return h
```

```python
import functools
import math

import jax
import jax.numpy as jnp
from jax import lax
from jax.experimental import pallas as pl
from jax.experimental.pallas import tpu as pltpu

D_MODEL = 1024
SEQ = 2048
N_META = 16
CONV_W = 256
CONV_K = 3
RET_HEADS = 4
RET_DK = 64
RET_W = RET_HEADS * RET_DK
MLA_HEADS = 8
MLA_NOPE = 64
MLA_ROPE = 32
MLA_V = 64
MLA_QK = MLA_NOPE + MLA_ROPE
Q_LORA = 256
KV_LORA = 128
D_FF = 4 * D_MODEL
ROPE_BASE = 10000.0
NORM_EPS = 1e-6

LANES = 128
BF16_ROWS = 16
HEAD_W = LANES
ATT_W = MLA_HEADS * HEAD_W
VT_ROWS = MLA_V + BF16_ROWS
VT_W = MLA_HEADS * VT_ROWS
MIX_HALF = CONV_W + RET_W
O_W = MLA_HEADS * MLA_V

TQ = 256
SUB = 2
STEP = SUB * TQ
PAD = STEP - N_META
TP = SEQ + STEP
NB = TP // TQ
NS = TP // STEP
FIRST_TILE = PAD // TQ

OFF_CB, OFF_CC, OFF_CH = 0, 256, 512
OFF_RQ, OFF_RK, OFF_RV, OFF_RG = 768, 1024, 1280, 1536
OFF_CQ, OFF_CKV, OFF_KR = 1792, 2048, 2176
D_IN_EXT = OFF_KR + HEAD_W

KV_UNROLL = 4
MASK_VALUE = -1e30
EXP2_SCALE = math.log2(math.e) / math.sqrt(MLA_QK)
MAX_SAFE_BOUND = 50.0
ROW_SUM_FLOOR = 1e-37
VMEM_LIMIT = 56 * 1024 * 1024

_F32 = jnp.float32
_BF16 = jnp.bfloat16
_NT_DIMS = (((1,), (1,)), ((), ()))
_TN_DIMS = (((0,), (0,)), ((), ()))


def _rms(x, n):
    return x * lax.rsqrt(jnp.sum(x * x, axis=-1, keepdims=True) * (1.0 / n) + NORM_EPS)


def _rot_pairs(z, half, first):
    w = z.shape[-1]
    return jnp.where(first, pltpu.roll(z, w - half, axis=1), pltpu.roll(z, half, axis=1))


def _load_step(first_layer, i, x_ref, meta_ref):
    x = x_ref[0]
    if first_layer:
        return jnp.where(i == 0, meta_ref[...], x)
    row = lax.broadcasted_iota(jnp.int32, (STEP, 1), 0)
    return jnp.where((i > 0) | (row >= PAD), x, 0.0)


def _pre_kernel(first_layer, *refs):
    if first_layer:
        x_ref, meta_ref = refs[:2]
        refs = refs[2:]
    else:
        x_ref, meta_ref = refs[0], None
        refs = refs[1:]
    consts = refs[:len(_PRE_CONSTS)]
    ycr_ref, q_ref, k_ref, vt_ref, ubuf, state, ycr0, q0, k0, vt0, halo0, state0 = refs[len(_PRE_CONSTS):]
    b = pl.program_id(0)
    i = pl.program_id(1)
    fresh = (b == 0) | (i > 0)

    @pl.when((b == 0) & (i == 0))
    def _():
        ubuf[0:8, :] = jnp.zeros((8, CONV_W), _F32)
        state[...] = jnp.zeros_like(state)

    @pl.when(fresh)
    def _():
        x = _load_step(first_layer, i, x_ref, meta_ref)
        for sub in range(SUB):
            _pre_tile(i, sub, x[sub * TQ:(sub + 1) * TQ], consts, ycr_ref, q_ref, k_ref, vt_ref, ubuf, state)

    @pl.when((b == 0) & (i == 0))
    def _():
        ycr0[...] = ycr_ref[0]
        q0[...] = q_ref[0]
        k0[...] = k_ref[0]
        vt0[...] = vt_ref[0]
        halo0[...] = ubuf[0:8, :]
        state0[...] = state[...]

    @pl.when(jnp.logical_not(fresh))
    def _():
        ycr_ref[0] = ycr0[...]
        q_ref[0] = q0[...]
        k_ref[0] = k0[...]
        vt_ref[0] = vt0[...]
        ubuf[0:8, :] = halo0[...]
        state[...] = state0[...]


def _pre_tile(i, sub, x, consts, ycr_ref, q_ref, k_ref, vt_ref, ubuf, state):
    (g_attn, w_in, conv_w, cosr, sinr, dmat, xi, zeta, cdec, bdmask, gmat, ret_g,
     qn_g, w_uq, w_uq_rot, kvn_g, w_k, w_vt, gpair, gq, gq_rot, gk, cosm, sinm, qk_bound) = consts
    rows = slice(sub * TQ, (sub + 1) * TQ)
    hn = (_rms(x, D_MODEL) * g_attn[...]).astype(_BF16)
    proj_m = jnp.dot(hn, w_in[:, OFF_CQ:D_IN_EXT], preferred_element_type=_F32)
    proj_r = jnp.dot(hn, w_in[:, OFF_RQ:OFF_CQ], preferred_element_type=_F32)

    cq = proj_m[:, 0:Q_LORA]
    ckv = proj_m[:, Q_LORA:Q_LORA + KV_LORA]
    kr128 = proj_m[:, Q_LORA + KV_LORA:]
    cqn = (_rms(cq, Q_LORA) * qn_g[...]).astype(_BF16)
    ckvn = (_rms(ckv, KV_LORA) * kvn_g[...]).astype(_BF16)
    qraw = jnp.dot(cqn, w_uq[...], preferred_element_type=_F32)
    qrot = jnp.dot(cqn, w_uq_rot[...], preferred_element_type=_F32)
    kraw = jnp.dot(ckvn, w_k[...], preferred_element_type=_F32)
    vt = lax.dot_general(w_vt[...], ckvn, _NT_DIMS, preferred_element_type=_F32)
    vrow = lax.broadcasted_iota(jnp.int32, (VT_W, 1), 0)
    vt_ref[0, sub] = jnp.where((vrow % VT_ROWS) >= MLA_V, 1.0, vt).astype(_BF16)

    lane = lax.broadcasted_iota(jnp.int32, (1, RET_W), 1)
    first = (lane % RET_DK) < (RET_DK // 2)
    rq = proj_r[:, 0:RET_W]
    rk = proj_r[:, RET_W:2 * RET_W]
    rv = proj_r[:, 2 * RET_W:3 * RET_W]
    rg = proj_r[:, 3 * RET_W:]
    cr, sr = cosr[rows, :], sinr[rows, :]
    qr = rq * cr + _rot_pairs(rq, RET_DK // 2, first) * sr
    kr = (rk * cr + _rot_pairs(rk, RET_DK // 2, first) * sr) * (RET_DK ** -0.5)
    kb = kr.astype(_BF16)
    hmasks = [(lane // RET_DK) == h for h in range(RET_HEADS)]
    scs = [lax.dot_general(jnp.where(hmasks[h], qr, 0.0).astype(_BF16), kb, _NT_DIMS,
                           preferred_element_type=_F32) for h in range(RET_HEADS)]

    g2 = gpair[...]
    q2 = (qraw * qraw).astype(_BF16)
    k2 = (kraw * kraw).astype(_BF16)
    kr_ms = jnp.dot((kr128 * kr128).astype(_BF16), g2[0:HEAD_W, 0:HEAD_W], preferred_element_type=_F32)
    kr_ms2 = jnp.concatenate([kr_ms, kr_ms], axis=1)
    rq2, rk2 = [], []
    for pr in range(MLA_HEADS // 2):
        psl = slice(2 * pr * HEAD_W, (2 * pr + 2) * HEAD_W)
        rq2.append(lax.rsqrt(jnp.dot(q2[:, psl], g2, preferred_element_type=_F32) + NORM_EPS))
        rk2.append(lax.rsqrt(jnp.dot(k2[:, psl], g2, preferred_element_type=_F32) + kr_ms2 + NORM_EPS))

    proj_c = jnp.dot(hn, w_in[:, OFF_CB:OFF_RQ], preferred_element_type=_F32)

    st = state[...]
    cross = jnp.dot(qr.astype(_BF16), st.astype(_BF16), preferred_element_type=_F32) * xi[...]
    upd = lax.dot_general((kr * zeta[...]).astype(_BF16), rv.astype(_BF16), _TN_DIMS,
                          preferred_element_type=_F32)
    state[...] = cdec[...] * st + bdmask[...] * upd
    inner = jnp.zeros((TQ, RET_W), _F32)
    for h in range(RET_HEADS):
        sc = (scs[h] * dmat[h]).astype(_BF16)
        vh = jnp.where(hmasks[h], rv, 0.0).astype(_BF16)
        inner = inner + jnp.dot(sc, vh, preferred_element_type=_F32)
    o = inner + cross
    msq = jnp.dot((o * o).astype(_BF16), gmat[...], preferred_element_type=_F32)

    lane_h = lax.broadcasted_iota(jnp.int32, (1, HEAD_W), 1)
    first_m = lane_h < (MLA_NOPE + MLA_ROPE // 2)
    cm, sm = cosm[rows, :], sinm[rows, :]
    gkv = gk[...]
    q_cos, q_sin = gq[...] * cm, gq_rot[...] * sm
    krg = kr128 * gkv
    k_rope = krg * cm + _rot_pairs(krg, MLA_ROPE // 2, first_m) * sm
    one_q = jnp.where(lane_h == MLA_QK, 1.0, 0.0)
    rowg = i * STEP + sub * TQ + lax.broadcasted_iota(jnp.int32, (TQ, 1), 0)
    kbias = jnp.where(lane_h == MLA_QK, jnp.where(rowg < PAD, MASK_VALUE, -qk_bound[0]), 0.0)
    for h in range(MLA_HEADS):
        sl = slice(h * HEAD_W, (h + 1) * HEAD_W)
        rsl = slice((h % 2) * HEAD_W, (h % 2 + 1) * HEAD_W)
        q_ref[0, rows, sl] = (rq2[h // 2][:, rsl] * (qraw[:, sl] * q_cos + qrot[:, sl] * q_sin)
                              + one_q).astype(_BF16)
        k_ref[0, rows, sl] = (rk2[h // 2][:, rsl] * (kraw[:, sl] * gkv + k_rope) + kbias).astype(_BF16)

    cb = proj_c[:, OFF_CB:OFF_CB + CONV_W]
    u = proj_c[:, OFF_CC:OFF_CC + CONV_W] * proj_c[:, OFF_CH:OFF_CH + CONV_W]
    ubuf[8:8 + TQ, :] = u
    cw = conv_w[...]
    conv = cw[0:1] * ubuf[6:6 + TQ, :] + cw[1:2] * ubuf[7:7 + TQ, :] + cw[2:3] * u
    ubuf[0:8, :] = u[TQ - 8:, :]
    ycr_ref[0, rows, 0:CONV_W] = (cb * conv).astype(_BF16)

    o_n = o * lax.rsqrt(msq + NORM_EPS) * ret_g[...]
    gate = rg * (1.0 / (1.0 + jnp.exp(-rg)))
    ycr_ref[0, rows, CONV_W:MIX_HALF] = (gate * o_n).astype(_BF16)


def _attn_kernel(q_first, bounded_ref, q_ref, k_ref, vt_ref, ot_ref, acc_scr, m_scr, s_scr):
    for t in range(q_first):
        ot_ref[0, t] = jnp.zeros((O_W, TQ), _BF16)
    bounded = bounded_ref[0] != 0
    run_all = functools.partial(_attn_pipeline, q_first, q_ref, k_ref, vt_ref, ot_ref, acc_scr, m_scr, s_scr)
    pl.when(bounded)(functools.partial(run_all, True))
    pl.when(jnp.logical_not(bounded))(functools.partial(run_all, False))


def _attn_pipeline(q_first, q_ref, k_ref, vt_ref, ot_ref, acc_scr, m_scr, s_scr, bounded):
    causal = (lax.broadcasted_iota(jnp.int32, (TQ, 1), 0) <= lax.broadcasted_iota(jnp.int32, (1, TQ), 1))
    heads = [slice(h * HEAD_W, (h + 1) * HEAD_W) for h in range(MLA_HEADS)]
    vheads = [slice(h * VT_ROWS, (h + 1) * VT_ROWS) for h in range(MLA_HEADS)]

    def tile_start(t):
        return t * TQ if isinstance(t, int) else pl.multiple_of(t * TQ, TQ)

    def scores(i, j, h):
        qs, ks = tile_start(i), tile_start(j)
        return lax.dot_general(k_ref[0, pl.ds(ks, TQ), heads[h]], q_ref[0, pl.ds(qs, TQ), heads[h]], _NT_DIMS,
                               preferred_element_type=_F32)

    def stage(i, j, nxt, diagonal):
        for h in range(MLA_HEADS):
            s = s_scr[h]
            s_scr[h] = scores(nxt[0], nxt[1], h)
            if diagonal:
                s = jnp.where(causal, s, MASK_VALUE)
            vth = vt_ref[0, j, vheads[h], :]
            if bounded:
                acc_scr[h] = acc_scr[h] + jnp.dot(vth, jnp.exp2(s).astype(_BF16), preferred_element_type=_F32)
                continue
            m_prev = m_scr[h]
            m_new = jnp.maximum(m_prev, jnp.max(s, axis=0, keepdims=True))
            p = jnp.exp2(s - m_new)
            alpha = jnp.exp2(m_prev - m_new)
            pv = jnp.dot(vth, p.astype(_BF16), preferred_element_type=_F32)
            acc_scr[h] = acc_scr[h] * alpha + pv
            m_scr[h] = m_new

    for h in range(MLA_HEADS):
        s_scr[h] = scores(q_first, FIRST_TILE, h)

    def tile_body(i, carry):
        if not bounded:
            m_scr[...] = jnp.full(m_scr.shape, MASK_VALUE, _F32)
        acc_scr[...] = jnp.zeros(acc_scr.shape, _F32)
        n_full = i - FIRST_TILE

        def run(j0, n):
            for u in range(n):
                stage(i, j0 + u, (i, j0 + u + 1), False)

        def unrolled_body(jj, c):
            run(FIRST_TILE + jj * KV_UNROLL, KV_UNROLL)
            return c

        def finish(rem):
            run(i - rem, rem)
            stage(i, i, (jnp.minimum(i + 1, NB - 1), FIRST_TILE), True)
            for h in range(MLA_HEADS):
                a = acc_scr[h]
                inv = 1.0 / jnp.maximum(a[MLA_V:MLA_V + 8], ROW_SUM_FLOOR)
                ot_ref[0, i, h * MLA_V:(h + 1) * MLA_V, :] = (
                    a[0:MLA_V] * jnp.concatenate([inv] * (MLA_V // 8), axis=0)).astype(_BF16)

        lax.fori_loop(0, n_full // KV_UNROLL, unrolled_body, 0)
        for rem in range(KV_UNROLL):
            pl.when(n_full % KV_UNROLL == rem)(functools.partial(finish, rem))
        return carry

    lax.fori_loop(q_first, NB, tile_body, 0)


def _post_step(first_layer, last_layer, x_ref, meta_ref, ycr_ref, ot_ref, w_out, g_mlp, w1, w2, out_ref):
    if last_layer:
        x = x_ref[0]
    else:
        x = _load_step(first_layer, pl.program_id(1), x_ref, meta_ref)
    tiles = [slice(t * TQ, (t + 1) * TQ) for t in range(SUB)]
    x1 = [x[r] + jnp.dot(ycr_ref[0, r, :], w_out[0:MIX_HALF, :], preferred_element_type=_F32)
          + lax.dot_general(ot_ref[0, t], w_out[MIX_HALF:, :], _TN_DIMS, preferred_element_type=_F32)
          for t, r in enumerate(tiles)]
    up = [jnp.dot((_rms(v, D_MODEL) * g_mlp[...]).astype(_BF16), w1[...], preferred_element_type=_F32)
          for v in x1]
    down = [jnp.dot(jnp.square(jnp.maximum(u, 0.0)).astype(_BF16), w2[...], preferred_element_type=_F32)
            for u in up]
    for t, r in enumerate(tiles):
        out_ref[0, r, :] = x1[t] + down[t]


def _post_kernel(first_layer, last_layer, *refs):
    if first_layer:
        x_ref, meta_ref = refs[:2]
        refs = refs[2:]
    else:
        x_ref, meta_ref = refs[0], None
        refs = refs[1:]
    if last_layer:
        _post_step(first_layer, last_layer, x_ref, meta_ref, *refs)
        return
    ycr_ref, ot_ref, w_out, g_mlp, w1, w2, out_ref, h0 = refs
    b = pl.program_id(0)
    i = pl.program_id(1)
    fresh = (b == 0) | (i > 0)

    @pl.when(fresh)
    def _():
        _post_step(first_layer, last_layer, x_ref, meta_ref, ycr_ref, ot_ref, w_out, g_mlp, w1, w2, out_ref)

    @pl.when((b == 0) & (i == 0))
    def _():
        h0[...] = out_ref[0]

    @pl.when(jnp.logical_not(fresh))
    def _():
        out_ref[0] = h0[...]


def _const_spec(shape):
    nd = len(shape)
    return pl.BlockSpec(shape, lambda b, i, _nd=nd: (0,) * _nd, pipeline_mode=pl.Buffered(1))


def _params(n_axes=2):
    return pltpu.CompilerParams(dimension_semantics=("arbitrary",) * n_axes, vmem_limit_bytes=VMEM_LIMIT)


def _rope_partner(lh):
    half = MLA_ROPE // 2
    return jnp.where((lh >= MLA_NOPE) & (lh < MLA_NOPE + half), lh + half,
                     jnp.where((lh >= MLA_NOPE + half) & (lh < MLA_QK), lh - half, lh))


def _tables():
    pos = jnp.maximum(jnp.arange(TP, dtype=_F32) - PAD, 0.0)
    inv_r = 1.0 / (ROPE_BASE ** (jnp.arange(0, RET_DK, 2, dtype=_F32) / RET_DK))
    ang_r = pos[:, None] * inv_r[None, :]
    l = jnp.arange(RET_W)
    cosr = jnp.cos(ang_r)[:, l % (RET_DK // 2)]
    sinr = jnp.sin(ang_r)[:, l % (RET_DK // 2)] * jnp.where((l % RET_DK) < RET_DK // 2, -1.0, 1.0)[None, :]
    inv_m = 1.0 / (ROPE_BASE ** (jnp.arange(0, MLA_ROPE, 2, dtype=_F32) / MLA_ROPE))
    ang_m = pos[:, None] * inv_m[None, :]
    lh = jnp.arange(HEAD_W)
    in_rope = (lh >= MLA_NOPE) & (lh < MLA_QK)
    fidx = jnp.clip(lh - MLA_NOPE, 0, MLA_ROPE - 1) % (MLA_ROPE // 2)
    cosm = jnp.where(in_rope[None, :], jnp.cos(ang_m)[:, fidx], 1.0)
    sgn = jnp.where(lh < MLA_NOPE + MLA_ROPE // 2, -1.0, 1.0)
    sinm = jnp.where(in_rope[None, :], jnp.sin(ang_m)[:, fidx] * sgn[None, :], 0.0)
    log_g = jnp.log1p(-jnp.exp2(-5.0 - jnp.arange(RET_HEADS, dtype=_F32)))
    idx = jnp.arange(TQ, dtype=_F32)
    diff = idx[:, None] - idx[None, :]
    dmat = jnp.where(diff >= 0, jnp.exp(jnp.maximum(diff, 0.0)[None] * log_g[:, None, None]), 0.0)
    lg_lane = log_g[l // RET_DK]
    xi = jnp.exp((idx[:, None] + 1.0) * lg_lane[None, :])
    zeta = jnp.exp((TQ - 1.0 - idx[:, None]) * lg_lane[None, :])
    cdec = jnp.exp(TQ * lg_lane)[None, :]
    same = (l[:, None] // RET_DK) == (l[None, :] // RET_DK)
    bdmask = same.astype(_F32)
    gmat = (same.astype(_F32) / RET_DK).astype(_BF16)
    l2 = jnp.arange(2 * HEAD_W)
    gpair = (((l2[:, None] // HEAD_W) == (l2[None, :] // HEAD_W)).astype(_F32) / MLA_QK).astype(_BF16)
    return dict(cosr=cosr, sinr=sinr, cosm=cosm, sinm=sinm, dmat=dmat, xi=xi, zeta=zeta, cdec=cdec,
                bdmask=bdmask, gmat=gmat, gpair=gpair)


def _layer_weights(l, attn_norm_g, w_in, conv_w, ret_gn_g, q_norm_g, w_uq, kv_norm_g, w_ukv,
                   q_head_norm_g, k_head_norm_g, w_out, mlp_norm_g, w_mlp_in, w_mlp_out):
    wi = w_in[l]
    w_in_ext = jnp.concatenate(
        [wi[:, :OFF_KR], jnp.zeros((D_MODEL, MLA_NOPE), _F32), wi[:, OFF_KR:],
         jnp.zeros((D_MODEL, HEAD_W - MLA_QK), _F32)], axis=1).astype(_BF16)
    partner = _rope_partner(jnp.arange(HEAD_W))
    is_rope = partner != jnp.arange(HEAD_W)
    wq = jnp.pad(w_uq[l].reshape(Q_LORA, MLA_HEADS, MLA_QK), ((0, 0), (0, 0), (0, HEAD_W - MLA_QK)))
    wq_rot = jnp.where(is_rope[None, None, :], wq[:, :, partner], 0.0)
    wkv = w_ukv[l].reshape(KV_LORA, MLA_HEADS, MLA_NOPE + MLA_V)
    w_k = jnp.pad(wkv[:, :, :MLA_NOPE], ((0, 0), (0, 0), (0, HEAD_W - MLA_NOPE))).reshape(KV_LORA, ATT_W)
    w_vt = jnp.pad(wkv[:, :, MLA_NOPE:], ((0, 0), (0, 0), (0, VT_ROWS - MLA_V))).reshape(KV_LORA, VT_W).T
    gq128 = jnp.pad(q_head_norm_g[l], (0, HEAD_W - MLA_QK)) * EXP2_SCALE
    gk128 = jnp.pad(k_head_norm_g[l], (0, HEAD_W - MLA_QK))
    qk_bound = 1.01 * MLA_QK * jnp.max(jnp.abs(gq128)) * jnp.max(jnp.abs(gk128))
    return dict(
        qk_bound=qk_bound.reshape(1).astype(_F32),
        bounded=(qk_bound <= MAX_SAFE_BOUND).reshape(1).astype(jnp.int32),
        g_attn=attn_norm_g[l][None, :], w_in=w_in_ext, conv_w=conv_w[l], ret_g=ret_gn_g[l].reshape(1, RET_W),
        qn_g=q_norm_g[l][None, :], w_uq=wq.reshape(Q_LORA, ATT_W).astype(_BF16),
        w_uq_rot=wq_rot.reshape(Q_LORA, ATT_W).astype(_BF16), kvn_g=kv_norm_g[l][None, :],
        w_k=w_k.astype(_BF16), w_vt=w_vt.astype(_BF16), gq=gq128[None, :],
        gq_rot=jnp.where(is_rope, gq128[partner], 0.0)[None, :], gk=gk128[None, :],
        w_out=w_out[l].astype(_BF16), g_mlp=mlp_norm_g[l][None, :], w1=w_mlp_in[l].astype(_BF16),
        w2=w_mlp_out[l].astype(_BF16))


def _x_specs(first_layer):
    if first_layer:
        return [pl.BlockSpec((1, STEP, D_MODEL), lambda b, i: (b, jnp.maximum(i - 1, 0), 0)),
                _const_spec((STEP, D_MODEL))]
    return [pl.BlockSpec((1, STEP, D_MODEL), lambda b, i: (b, i, 0))]


_PRE_CONSTS = [("g_attn", "w"), ("w_in", "w"), ("conv_w", "w"), ("cosr", "row"), ("sinr", "row"), ("dmat", "t"),
               ("xi", "t"), ("zeta", "t"), ("cdec", "t"), ("bdmask", "t"), ("gmat", "t"), ("ret_g", "w"),
               ("qn_g", "w"), ("w_uq", "w"), ("w_uq_rot", "w"), ("kvn_g", "w"), ("w_k", "w"), ("w_vt", "w"),
               ("gpair", "t"), ("gq", "w"), ("gq_rot", "w"), ("gk", "w"), ("cosm", "row"), ("sinm", "row"),
               ("qk_bound", "smem")]


def _pre_call(first_layer, batch, x_args, w, t):
    step = lambda width: pl.BlockSpec((1, STEP, width), lambda b, i: (b, i, 0))
    in_specs = _x_specs(first_layer)
    args = list(x_args)
    for name, kind in _PRE_CONSTS:
        arr = t[name] if kind in ("t", "row") else w[name]
        if kind == "row":
            in_specs.append(pl.BlockSpec((STEP, arr.shape[1]), lambda b, i: (i, 0)))
        elif kind == "smem":
            in_specs.append(pl.BlockSpec(memory_space=pltpu.SMEM))
        else:
            in_specs.append(_const_spec(arr.shape))
        args.append(arr)
    out_shape = [jax.ShapeDtypeStruct((batch, TP, MIX_HALF), _BF16),
                 jax.ShapeDtypeStruct((batch, TP, ATT_W), _BF16),
                 jax.ShapeDtypeStruct((batch, TP, ATT_W), _BF16),
                 jax.ShapeDtypeStruct((batch, NB, VT_W, TQ), _BF16)]
    scratch = [pltpu.VMEM((TQ + 8, CONV_W), _F32), pltpu.VMEM((RET_W, RET_W), _F32),
               pltpu.VMEM((STEP, MIX_HALF), _BF16), pltpu.VMEM((STEP, ATT_W), _BF16),
               pltpu.VMEM((STEP, ATT_W), _BF16), pltpu.VMEM((SUB, VT_W, TQ), _BF16),
               pltpu.VMEM((8, CONV_W), _F32), pltpu.VMEM((RET_W, RET_W), _F32)]
    return pl.pallas_call(
        functools.partial(_pre_kernel, first_layer),
        grid=(batch, NS), in_specs=in_specs,
        out_specs=[step(MIX_HALF), step(ATT_W), step(ATT_W),
                   pl.BlockSpec((1, SUB, VT_W, TQ), lambda b, i: (b, i, 0, 0))],
        out_shape=out_shape, scratch_shapes=scratch, compiler_params=_params(), name="pre")(*args)


def _attn_call(last_layer, batch, bounded, q, k, vt):
    q_first = FIRST_TILE + 1 if last_layer else FIRST_TILE
    return pl.pallas_call(
        functools.partial(_attn_kernel, q_first), grid=(batch,),
        in_specs=[pl.BlockSpec(memory_space=pltpu.SMEM),
                  pl.BlockSpec((1, TP, ATT_W), lambda b: (b, 0, 0)),
                  pl.BlockSpec((1, TP, ATT_W), lambda b: (b, 0, 0)),
                  pl.BlockSpec((1, NB, VT_W, TQ), lambda b: (b, 0, 0, 0))],
        out_specs=pl.BlockSpec((1, NB, O_W, TQ), lambda b: (b, 0, 0, 0)),
        out_shape=jax.ShapeDtypeStruct((batch, NB, O_W, TQ), _BF16),
        scratch_shapes=[pltpu.VMEM((MLA_HEADS, VT_ROWS, TQ), _F32), pltpu.VMEM((MLA_HEADS, 1, TQ), _F32),
                        pltpu.VMEM((MLA_HEADS, TQ, TQ), _F32)],
        compiler_params=_params(1), name="attn")(bounded, q, k, vt)


def _post_call(first_layer, last_layer, batch, x_args, ycr, ot, w):
    off = 1 if last_layer else 0
    if last_layer:
        x_specs = [pl.BlockSpec((1, STEP, D_MODEL), lambda b, i: (b, i + off, 0))]
        out_shape = jax.ShapeDtypeStruct((batch, SEQ, D_MODEL), _F32)
        n_steps = NS - 1
        scratch = []
    else:
        x_specs = _x_specs(first_layer)
        out_shape = jax.ShapeDtypeStruct((batch, TP, D_MODEL), _F32)
        n_steps = NS
        scratch = [pltpu.VMEM((STEP, D_MODEL), _F32)]
    const_names = ["w_out", "g_mlp", "w1", "w2"]
    return pl.pallas_call(
        functools.partial(_post_kernel, first_layer, last_layer),
        grid=(batch, n_steps),
        in_specs=x_specs + [pl.BlockSpec((1, STEP, MIX_HALF), lambda b, i: (b, i + off, 0)),
                            pl.BlockSpec((1, SUB, O_W, TQ), lambda b, i: (b, i + off, 0, 0))]
        + [_const_spec(w[n].shape) for n in const_names],
        out_specs=pl.BlockSpec((1, STEP, D_MODEL), lambda b, i: (b, i, 0)),
        out_shape=out_shape, scratch_shapes=scratch, compiler_params=_params(), name="post")(
            *x_args, ycr, ot, *[w[n] for n in const_names])


def kernel(x, meta_tokens, attn_norm_g, w_in, conv_w, ret_gn_g, q_norm_g, w_uq, kv_norm_g, w_ukv,
           q_head_norm_g, k_head_norm_g, w_out, mlp_norm_g, w_mlp_in, w_mlp_out):
    batch = x.shape[0]
    depth = w_in.shape[0]
    assert x.shape[1:] == (SEQ, D_MODEL) and depth >= 2
    t = _tables()
    meta_pad = jnp.concatenate([jnp.zeros((PAD, D_MODEL), x.dtype), meta_tokens.astype(x.dtype)], axis=0)
    h = None
    for l in range(depth):
        w = _layer_weights(l, attn_norm_g, w_in, conv_w, ret_gn_g, q_norm_g, w_uq, kv_norm_g, w_ukv,
                           q_head_norm_g, k_head_norm_g, w_out, mlp_norm_g, w_mlp_in, w_mlp_out)
        first, last = l == 0, l == depth - 1
        x_args = (x, meta_pad) if first else (h,)
        ycr, q, k, vt = _pre_call(first, batch, x_args, w, t)
        ot = _attn_call(last, batch, w["bounded"], q, k, vt)
        h = _post_call(first, last, batch, x_args, ycr, ot, w)
    return h
```

```python
import functools
import math

import jax
import jax.numpy as jnp
from jax import lax
from jax.experimental import pallas as pl
from jax.experimental.pallas import tpu as pltpu

D_MODEL = 1024
SEQ = 2048
N_META = 16
CONV_W = 256
CONV_K = 3
RET_HEADS = 4
RET_DK = 64
RET_W = RET_HEADS * RET_DK
MLA_HEADS = 8
MLA_NOPE = 64
MLA_ROPE = 32
MLA_V = 64
MLA_QK = MLA_NOPE + MLA_ROPE
Q_LORA = 256
KV_LORA = 128
D_FF = 4 * D_MODEL
ROPE_BASE = 10000.0
NORM_EPS = 1e-6

LANES = 128
BF16_ROWS = 16
HEAD_W = LANES
ATT_W = MLA_HEADS * HEAD_W
VT_ROWS = MLA_V + BF16_ROWS
VT_W = MLA_HEADS * VT_ROWS
MIX_HALF = CONV_W + RET_W
O_W = MLA_HEADS * MLA_V

TQ = 256
SUB = 2
STEP = SUB * TQ
PAD = STEP - N_META
META_TILE = PAD // TQ
NRS = SEQ // STEP
NRT = SEQ // TQ

OFF_CB, OFF_CC, OFF_CH = 0, 256, 512
OFF_RQ, OFF_RK, OFF_RV, OFF_RG = 768, 1024, 1280, 1536
OFF_CQ, OFF_CKV, OFF_KR = 1792, 2048, 2176
D_IN_EXT = OFF_KR + HEAD_W

KV_UNROLL = 4
MASK_VALUE = -1e30
EXP2_SCALE = math.log2(math.e) / math.sqrt(MLA_QK)
MAX_SAFE_BOUND = 50.0
VMEM_LIMIT = 56 * 1024 * 1024

_F32 = jnp.float32
_BF16 = jnp.bfloat16
_NT_DIMS = (((1,), (1,)), ((), ()))
_TN_DIMS = (((0,), (0,)), ((), ()))


def _rms(x, n):
    return x * lax.rsqrt(jnp.sum(x * x, axis=-1, keepdims=True) * (1.0 / n) + NORM_EPS)


def _rot_pairs(z, half, first):
    w = z.shape[-1]
    return jnp.where(first, pltpu.roll(z, w - half, axis=1), pltpu.roll(z, half, axis=1))


def _meta_rows(xm):
    row = lax.broadcasted_iota(jnp.int32, (STEP, 1), 0)
    return jnp.where(row >= PAD, xm, 0.0)


def _pre_kernel(x_ref, xm_ref, *refs):
    consts = refs[:len(_PRE_CONSTS)]
    (ycr_ref, q_ref, k_ref, vt_ref, ycr_m, q_m, k_m, vt_m,
     ubuf, state, halo0, state0) = refs[len(_PRE_CONSTS):]
    s = pl.program_id(0)

    @pl.when(s == 0)
    def _():
        ubuf[0:8, :] = jnp.zeros((8, CONV_W), _F32)
        state[...] = jnp.zeros_like(state)
        x = _meta_rows(xm_ref[...])
        for sub in range(SUB):
            _pre_tile(True, sub, x[sub * TQ:(sub + 1) * TQ], consts, ycr_m, q_m, k_m, vt_m, ubuf, state)
        halo0[...] = ubuf[0:8, :]
        state0[...] = state[...]

    @pl.when(s > 0)
    def _():
        @pl.when((s - 1) % NRS == 0)
        def _():
            ubuf[0:8, :] = halo0[...]
            state[...] = state0[...]

        x = x_ref[0]
        for sub in range(SUB):
            _pre_tile(False, sub, x[sub * TQ:(sub + 1) * TQ], consts, ycr_ref, q_ref, k_ref, vt_ref, ubuf, state)


def _pre_tile(meta, sub, x, consts, ycr_ref, q_ref, k_ref, vt_ref, ubuf, state):
    (g_attn, w_in, conv_w, cosr, sinr, dmat, xi, zeta, cdec, bdmask, gmat, ret_g,
     qn_g, w_uq, w_uq_rot, kvn_g, w_k, w_vt, gpair, gq, gq_rot, gk, cosm, sinm, qk_bound) = consts
    rows = slice(sub * TQ, (sub + 1) * TQ)
    hn = (_rms(x, D_MODEL) * g_attn[...]).astype(_BF16)
    proj_m = jnp.dot(hn, w_in[:, OFF_CQ:D_IN_EXT], preferred_element_type=_F32)
    proj_r = jnp.dot(hn, w_in[:, OFF_RQ:OFF_CQ], preferred_element_type=_F32)

    cq = proj_m[:, 0:Q_LORA]
    ckv = proj_m[:, Q_LORA:Q_LORA + KV_LORA]
    kr128 = proj_m[:, Q_LORA + KV_LORA:]
    cqn = (_rms(cq, Q_LORA) * qn_g[...]).astype(_BF16)
    ckvn = (_rms(ckv, KV_LORA) * kvn_g[...]).astype(_BF16)
    qraw = jnp.dot(cqn, w_uq[...], preferred_element_type=_F32)
    qrot = jnp.dot(cqn, w_uq_rot[...], preferred_element_type=_F32)
    kraw = jnp.dot(ckvn, w_k[...], preferred_element_type=_F32)
    vt = lax.dot_general(w_vt[...], ckvn, _NT_DIMS, preferred_element_type=_F32)
    vrow = lax.broadcasted_iota(jnp.int32, (VT_W, 1), 0)
    vt_ref[0, sub] = jnp.where((vrow % VT_ROWS) >= MLA_V, 1.0, vt).astype(_BF16)

    lane = lax.broadcasted_iota(jnp.int32, (1, RET_W), 1)
    first = (lane % RET_DK) < (RET_DK // 2)
    rq = proj_r[:, 0:RET_W]
    rk = proj_r[:, RET_W:2 * RET_W]
    rv = proj_r[:, 2 * RET_W:3 * RET_W]
    rg = proj_r[:, 3 * RET_W:]
    cr, sr = cosr[rows, :], sinr[rows, :]
    qr = rq * cr + _rot_pairs(rq, RET_DK // 2, first) * sr
    kr = (rk * cr + _rot_pairs(rk, RET_DK // 2, first) * sr) * (RET_DK ** -0.5)
    kb = kr.astype(_BF16)
    hmasks = [(lane // RET_DK) == h for h in range(RET_HEADS)]
    scs = [lax.dot_general(jnp.where(hmasks[h], qr, 0.0).astype(_BF16), kb, _NT_DIMS,
                           preferred_element_type=_F32) for h in range(RET_HEADS)]

    g2 = gpair[...]
    q2 = (qraw * qraw).astype(_BF16)
    k2 = (kraw * kraw).astype(_BF16)
    kr_ms = jnp.dot((kr128 * kr128).astype(_BF16), g2[0:HEAD_W, 0:HEAD_W], preferred_element_type=_F32)
    kr_ms2 = jnp.concatenate([kr_ms, kr_ms], axis=1)
    rq2, rk2 = [], []
    for pr in range(MLA_HEADS // 2):
        psl = slice(2 * pr * HEAD_W, (2 * pr + 2) * HEAD_W)
        rq2.append(lax.rsqrt(jnp.dot(q2[:, psl], g2, preferred_element_type=_F32) + NORM_EPS))
        rk2.append(lax.rsqrt(jnp.dot(k2[:, psl], g2, preferred_element_type=_F32) + kr_ms2 + NORM_EPS))

    proj_c = jnp.dot(hn, w_in[:, OFF_CB:OFF_RQ], preferred_element_type=_F32)

    st = state[...]
    cross = jnp.dot(qr.astype(_BF16), st.astype(_BF16), preferred_element_type=_F32) * xi[...]
    upd = lax.dot_general((kr * zeta[...]).astype(_BF16), rv.astype(_BF16), _TN_DIMS,
                          preferred_element_type=_F32)
    state[...] = cdec[...] * st + bdmask[...] * upd
    inner = jnp.zeros((TQ, RET_W), _F32)
    for h in range(RET_HEADS):
        sc = (scs[h] * dmat[h]).astype(_BF16)
        vh = jnp.where(hmasks[h], rv, 0.0).astype(_BF16)
        inner = inner + jnp.dot(sc, vh, preferred_element_type=_F32)
    o = inner + cross
    msq = jnp.dot((o * o).astype(_BF16), gmat[...], preferred_element_type=_F32)

    lane_h = lax.broadcasted_iota(jnp.int32, (1, HEAD_W), 1)
    first_m = lane_h < (MLA_NOPE + MLA_ROPE // 2)
    cm, sm = cosm[rows, :], sinm[rows, :]
    gkv = gk[...]
    q_cos, q_sin = gq[...] * cm, gq_rot[...] * sm
    krg = kr128 * gkv
    k_rope = krg * cm + _rot_pairs(krg, MLA_ROPE // 2, first_m) * sm
    one_q = jnp.where(lane_h == MLA_QK, 1.0, 0.0)
    kbias = jnp.where(lane_h == MLA_QK, -qk_bound[0], 0.0)
    if meta:
        rowm = sub * TQ + lax.broadcasted_iota(jnp.int32, (TQ, 1), 0)
        kbias = jnp.where((rowm < PAD) & (lane_h == MLA_QK), MASK_VALUE, kbias)
    for h in range(MLA_HEADS):
        sl = slice(h * HEAD_W, (h + 1) * HEAD_W)
        rsl = slice((h % 2) * HEAD_W, (h % 2 + 1) * HEAD_W)
        q_ref[0, rows, sl] = (rq2[h // 2][:, rsl] * (qraw[:, sl] * q_cos + qrot[:, sl] * q_sin)
                              + one_q).astype(_BF16)
        k_ref[0, rows, sl] = (rk2[h // 2][:, rsl] * (kraw[:, sl] * gkv + k_rope) + kbias).astype(_BF16)

    cb = proj_c[:, OFF_CB:OFF_CB + CONV_W]
    u = proj_c[:, OFF_CC:OFF_CC + CONV_W] * proj_c[:, OFF_CH:OFF_CH + CONV_W]
    ubuf[8:8 + TQ, :] = u
    cw = conv_w[...]
    conv = cw[0:1] * ubuf[6:6 + TQ, :] + cw[1:2] * ubuf[7:7 + TQ, :] + cw[2:3] * u
    ubuf[0:8, :] = u[TQ - 8:, :]
    ycr_ref[0, rows, 0:CONV_W] = (cb * conv).astype(_BF16)

    o_n = o * lax.rsqrt(msq + NORM_EPS) * ret_g[...]
    gate = rg * (1.0 / (1.0 + jnp.exp(-rg)))
    ycr_ref[0, rows, CONV_W:MIX_HALF] = (gate * o_n).astype(_BF16)


def _attn_kernel(bounded_ref, q_ref, k_ref, vt_ref, km_ref, vtm_ref, ot_ref, acc_scr, m_scr, s_scr):
    bounded = bounded_ref[0] != 0
    run_all = functools.partial(_attn_pipeline, q_ref, k_ref, vt_ref, km_ref, vtm_ref, ot_ref, acc_scr, m_scr,
                                s_scr)
    pl.when(bounded)(functools.partial(run_all, True))
    pl.when(jnp.logical_not(bounded))(functools.partial(run_all, False))


_META_BLOCK = "meta"


def _attn_pipeline(q_ref, k_ref, vt_ref, km_ref, vtm_ref, ot_ref, acc_scr, m_scr, s_scr, bounded):
    causal = (lax.broadcasted_iota(jnp.int32, (TQ, 1), 0) <= lax.broadcasted_iota(jnp.int32, (1, TQ), 1))
    heads = [slice(h * HEAD_W, (h + 1) * HEAD_W) for h in range(MLA_HEADS)]
    vheads = [slice(h * VT_ROWS, (h + 1) * VT_ROWS) for h in range(MLA_HEADS)]

    def tile_start(t):
        return t * TQ if isinstance(t, int) else pl.multiple_of(t * TQ, TQ)

    def scores(i, j, h):
        if j is _META_BLOCK:
            keys = km_ref[0, :, heads[h]]
        else:
            keys = k_ref[0, pl.ds(tile_start(j), TQ), heads[h]]
        return lax.dot_general(keys, q_ref[0, pl.ds(tile_start(i), TQ), heads[h]], _NT_DIMS,
                               preferred_element_type=_F32)

    def stage(j, nxt, diagonal=False):
        for h in range(MLA_HEADS):
            s = s_scr[h]
            s_scr[h] = scores(nxt[0], nxt[1], h)
            if diagonal:
                s = jnp.where(causal, s, MASK_VALUE)
            vth = vtm_ref[0, 0, vheads[h], :] if j is _META_BLOCK else vt_ref[0, j, vheads[h], :]
            if bounded:
                acc_scr[h] = acc_scr[h] + jnp.dot(vth, jnp.exp2(s).astype(_BF16), preferred_element_type=_F32)
                continue
            m_prev = m_scr[h]
            m_new = jnp.maximum(m_prev, jnp.max(s, axis=0, keepdims=True))
            p = jnp.exp2(s - m_new)
            alpha = jnp.exp2(m_prev - m_new)
            pv = jnp.dot(vth, p.astype(_BF16), preferred_element_type=_F32)
            acc_scr[h] = acc_scr[h] * alpha + pv
            m_scr[h] = m_new

    for h in range(MLA_HEADS):
        s_scr[h] = scores(0, 0, h)

    def tile_body(i, carry):
        if not bounded:
            m_scr[...] = jnp.full(m_scr.shape, MASK_VALUE, _F32)
        acc_scr[...] = jnp.zeros(acc_scr.shape, _F32)

        def run(j0, n):
            for u in range(n):
                stage(j0 + u, (i, j0 + u + 1))

        def unrolled_body(jj, c):
            run(jj * KV_UNROLL, KV_UNROLL)
            return c

        def finish(rem):
            run(i - rem, rem)
            stage(i, (i, _META_BLOCK), diagonal=True)
            nxt_tile = jnp.minimum(i + 1, NRT - 1)
            stage(_META_BLOCK, (nxt_tile, 0))
            for h in range(MLA_HEADS):
                a = acc_scr[h]
                inv = 1.0 / a[MLA_V:MLA_V + 8]
                ot_ref[0, i, h * MLA_V:(h + 1) * MLA_V, :] = (
                    a[0:MLA_V] * jnp.concatenate([inv] * (MLA_V // 8), axis=0)).astype(_BF16)

        lax.fori_loop(0, i // KV_UNROLL, unrolled_body, 0)
        for rem in range(KV_UNROLL):
            pl.when(i % KV_UNROLL == rem)(functools.partial(finish, rem))
        return carry

    lax.fori_loop(0, NRT, tile_body, 0)


def _attn_meta_kernel(q_ref, k_ref, vt_ref, ot_ref):
    causal = (lax.broadcasted_iota(jnp.int32, (TQ, 1), 0) <= lax.broadcasted_iota(jnp.int32, (1, TQ), 1))
    for t in range(SUB):
        if t != META_TILE:
            ot_ref[0, t] = jnp.zeros((O_W, TQ), _BF16)
    for h in range(MLA_HEADS):
        sl = slice(h * HEAD_W, (h + 1) * HEAD_W)
        s = lax.dot_general(k_ref[0, :, sl], q_ref[0, :, sl], _NT_DIMS, preferred_element_type=_F32)
        s = jnp.where(causal, s, MASK_VALUE)
        p = jnp.exp2(s - jnp.max(s, axis=0, keepdims=True))
        a = jnp.dot(vt_ref[0, 0, h * VT_ROWS:(h + 1) * VT_ROWS, :], p.astype(_BF16), preferred_element_type=_F32)
        inv = 1.0 / a[MLA_V:MLA_V + 8]
        ot_ref[0, META_TILE, h * MLA_V:(h + 1) * MLA_V, :] = (
            a[0:MLA_V] * jnp.concatenate([inv] * (MLA_V // 8), axis=0)).astype(_BF16)


def _post_step(x, ycr_ref, ot_ref, w_out, g_mlp, w1, w2, out_ref):
    tiles = [slice(t * TQ, (t + 1) * TQ) for t in range(SUB)]
    x1 = [x[r] + jnp.dot(ycr_ref[0, r, :], w_out[0:MIX_HALF, :], preferred_element_type=_F32)
          + lax.dot_general(ot_ref[0, t], w_out[MIX_HALF:, :], _TN_DIMS, preferred_element_type=_F32)
          for t, r in enumerate(tiles)]
    up = [jnp.dot((_rms(v, D_MODEL) * g_mlp[...]).astype(_BF16), w1[...], preferred_element_type=_F32)
          for v in x1]
    down = [jnp.dot(jnp.square(jnp.maximum(u, 0.0)).astype(_BF16), w2[...], preferred_element_type=_F32)
            for u in up]
    for t, r in enumerate(tiles):
        out_ref[0, r, :] = x1[t] + down[t]


def _post_kernel(with_meta, *refs):
    if not with_meta:
        x_ref, ycr_ref, ot_ref, w_out, g_mlp, w1, w2, out_ref = refs
        _post_step(x_ref[0], ycr_ref, ot_ref, w_out, g_mlp, w1, w2, out_ref)
        return
    x_ref, xm_ref, ycr_ref, ycr_m, ot_ref, ot_m, w_out, g_mlp, w1, w2, out_ref, out_m = refs
    s = pl.program_id(0)

    @pl.when(s == 0)
    def _():
        _post_step(_meta_rows(xm_ref[...]), ycr_m, ot_m, w_out, g_mlp, w1, w2, out_m)

    @pl.when(s > 0)
    def _():
        _post_step(x_ref[0], ycr_ref, ot_ref, w_out, g_mlp, w1, w2, out_ref)


def _const_spec(shape):
    nd = len(shape)
    return pl.BlockSpec(shape, lambda s, _nd=nd: (0,) * _nd, pipeline_mode=pl.Buffered(1))


def _params():
    return pltpu.CompilerParams(dimension_semantics=("arbitrary",), vmem_limit_bytes=VMEM_LIMIT)


def _batch_step(s):
    r = jnp.maximum(s - 1, 0)
    return r // NRS, r % NRS


def _rope_partner(lh):
    half = MLA_ROPE // 2
    return jnp.where((lh >= MLA_NOPE) & (lh < MLA_NOPE + half), lh + half,
                     jnp.where((lh >= MLA_NOPE + half) & (lh < MLA_QK), lh - half, lh))


def _tables():
    pos = jnp.maximum(jnp.arange(STEP + SEQ, dtype=_F32) - PAD, 0.0)
    inv_r = 1.0 / (ROPE_BASE ** (jnp.arange(0, RET_DK, 2, dtype=_F32) / RET_DK))
    ang_r = pos[:, None] * inv_r[None, :]
    l = jnp.arange(RET_W)
    cosr = jnp.cos(ang_r)[:, l % (RET_DK // 2)]
    sinr = jnp.sin(ang_r)[:, l % (RET_DK // 2)] * jnp.where((l % RET_DK) < RET_DK // 2, -1.0, 1.0)[None, :]
    inv_m = 1.0 / (ROPE_BASE ** (jnp.arange(0, MLA_ROPE, 2, dtype=_F32) / MLA_ROPE))
    ang_m = pos[:, None] * inv_m[None, :]
    lh = jnp.arange(HEAD_W)
    in_rope = (lh >= MLA_NOPE) & (lh < MLA_QK)
    fidx = jnp.clip(lh - MLA_NOPE, 0, MLA_ROPE - 1) % (MLA_ROPE // 2)
    cosm = jnp.where(in_rope[None, :], jnp.cos(ang_m)[:, fidx], 1.0)
    sgn = jnp.where(lh < MLA_NOPE + MLA_ROPE // 2, -1.0, 1.0)
    sinm = jnp.where(in_rope[None, :], jnp.sin(ang_m)[:, fidx] * sgn[None, :], 0.0)
    log_g = jnp.log1p(-jnp.exp2(-5.0 - jnp.arange(RET_HEADS, dtype=_F32)))
    idx = jnp.arange(TQ, dtype=_F32)
    diff = idx[:, None] - idx[None, :]
    dmat = jnp.where(diff >= 0, jnp.exp(jnp.maximum(diff, 0.0)[None] * log_g[:, None, None]), 0.0)
    lg_lane = log_g[l // RET_DK]
    xi = jnp.exp((idx[:, None] + 1.0) * lg_lane[None, :])
    zeta = jnp.exp((TQ - 1.0 - idx[:, None]) * lg_lane[None, :])
    cdec = jnp.exp(TQ * lg_lane)[None, :]
    same = (l[:, None] // RET_DK) == (l[None, :] // RET_DK)
    bdmask = same.astype(_F32)
    gmat = (same.astype(_F32) / RET_DK).astype(_BF16)
    l2 = jnp.arange(2 * HEAD_W)
    gpair = (((l2[:, None] // HEAD_W) == (l2[None, :] // HEAD_W)).astype(_F32) / MLA_QK).astype(_BF16)
    return dict(cosr=cosr, sinr=sinr, cosm=cosm, sinm=sinm, dmat=dmat, xi=xi, zeta=zeta, cdec=cdec,
                bdmask=bdmask, gmat=gmat, gpair=gpair)


def _layer_weights(l, attn_norm_g, w_in, conv_w, ret_gn_g, q_norm_g, w_uq, kv_norm_g, w_ukv,
                   q_head_norm_g, k_head_norm_g, w_out, mlp_norm_g, w_mlp_in, w_mlp_out):
    wi = w_in[l]
    w_in_ext = jnp.concatenate(
        [wi[:, :OFF_KR], jnp.zeros((D_MODEL, MLA_NOPE), _F32), wi[:, OFF_KR:],
         jnp.zeros((D_MODEL, HEAD_W - MLA_QK), _F32)], axis=1).astype(_BF16)
    partner = _rope_partner(jnp.arange(HEAD_W))
    is_rope = partner != jnp.arange(HEAD_W)
    wq = jnp.pad(w_uq[l].reshape(Q_LORA, MLA_HEADS, MLA_QK), ((0, 0), (0, 0), (0, HEAD_W - MLA_QK)))
    wq_rot = jnp.where(is_rope[None, None, :], wq[:, :, partner], 0.0)
    wkv = w_ukv[l].reshape(KV_LORA, MLA_HEADS, MLA_NOPE + MLA_V)
    w_k = jnp.pad(wkv[:, :, :MLA_NOPE], ((0, 0), (0, 0), (0, HEAD_W - MLA_NOPE))).reshape(KV_LORA, ATT_W)
    w_vt = jnp.pad(wkv[:, :, MLA_NOPE:], ((0, 0), (0, 0), (0, VT_ROWS - MLA_V))).reshape(KV_LORA, VT_W).T
    gq128 = jnp.pad(q_head_norm_g[l], (0, HEAD_W - MLA_QK)) * EXP2_SCALE
    gk128 = jnp.pad(k_head_norm_g[l], (0, HEAD_W - MLA_QK))
    qk_bound = 1.01 * MLA_QK * jnp.max(jnp.abs(gq128)) * jnp.max(jnp.abs(gk128))
    return dict(
        qk_bound=qk_bound.reshape(1).astype(_F32),
        bounded=(qk_bound <= MAX_SAFE_BOUND).reshape(1).astype(jnp.int32),
        g_attn=attn_norm_g[l][None, :], w_in=w_in_ext, conv_w=conv_w[l], ret_g=ret_gn_g[l].reshape(1, RET_W),
        qn_g=q_norm_g[l][None, :], w_uq=wq.reshape(Q_LORA, ATT_W).astype(_BF16),
        w_uq_rot=wq_rot.reshape(Q_LORA, ATT_W).astype(_BF16), kvn_g=kv_norm_g[l][None, :],
        w_k=w_k.astype(_BF16), w_vt=w_vt.astype(_BF16), gq=gq128[None, :],
        gq_rot=jnp.where(is_rope, gq128[partner], 0.0)[None, :], gk=gk128[None, :],
        w_out=w_out[l].astype(_BF16), g_mlp=mlp_norm_g[l][None, :], w1=w_mlp_in[l].astype(_BF16),
        w2=w_mlp_out[l].astype(_BF16))


_PRE_CONSTS = [("g_attn", "w"), ("w_in", "w"), ("conv_w", "w"), ("cosr", "row"), ("sinr", "row"), ("dmat", "t"),
               ("xi", "t"), ("zeta", "t"), ("cdec", "t"), ("bdmask", "t"), ("gmat", "t"), ("ret_g", "w"),
               ("qn_g", "w"), ("w_uq", "w"), ("w_uq_rot", "w"), ("kvn_g", "w"), ("w_k", "w"), ("w_vt", "w"),
               ("gpair", "t"), ("gq", "w"), ("gq_rot", "w"), ("gk", "w"), ("cosm", "row"), ("sinm", "row"),
               ("qk_bound", "smem")]


def _main_step_spec(width):
    def index(s):
        b, i = _batch_step(s)
        return b, i, 0
    return pl.BlockSpec((1, STEP, width), index)


def _main_tiles_spec(rows):
    def index(s):
        b, i = _batch_step(s)
        return b, i, 0, 0
    return pl.BlockSpec((1, SUB, rows, TQ), index)


def _pre_call(batch, x, x_meta, w, t):
    in_specs = [_main_step_spec(D_MODEL), _const_spec((STEP, D_MODEL))]
    args = [x, x_meta]
    for name, kind in _PRE_CONSTS:
        arr = t[name] if kind in ("t", "row") else w[name]
        if kind == "row":
            in_specs.append(pl.BlockSpec((STEP, arr.shape[1]),
                                         lambda s: (jnp.where(s == 0, 0, _batch_step(s)[1] + 1), 0)))
        elif kind == "smem":
            in_specs.append(pl.BlockSpec(memory_space=pltpu.SMEM))
        else:
            in_specs.append(_const_spec(arr.shape))
        args.append(arr)
    out_shape = [jax.ShapeDtypeStruct((batch, SEQ, MIX_HALF), _BF16),
                 jax.ShapeDtypeStruct((batch, SEQ, ATT_W), _BF16),
                 jax.ShapeDtypeStruct((batch, SEQ, ATT_W), _BF16),
                 jax.ShapeDtypeStruct((batch, NRT, VT_W, TQ), _BF16),
                 jax.ShapeDtypeStruct((1, STEP, MIX_HALF), _BF16),
                 jax.ShapeDtypeStruct((1, STEP, ATT_W), _BF16),
                 jax.ShapeDtypeStruct((1, STEP, ATT_W), _BF16),
                 jax.ShapeDtypeStruct((1, SUB, VT_W, TQ), _BF16)]
    out_specs = [_main_step_spec(MIX_HALF), _main_step_spec(ATT_W), _main_step_spec(ATT_W),
                 _main_tiles_spec(VT_W),
                 pl.BlockSpec((1, STEP, MIX_HALF), lambda s: (0, 0, 0)),
                 pl.BlockSpec((1, STEP, ATT_W), lambda s: (0, 0, 0)),
                 pl.BlockSpec((1, STEP, ATT_W), lambda s: (0, 0, 0)),
                 pl.BlockSpec((1, SUB, VT_W, TQ), lambda s: (0, 0, 0, 0))]
    scratch = [pltpu.VMEM((TQ + 8, CONV_W), _F32), pltpu.VMEM((RET_W, RET_W), _F32),
               pltpu.VMEM((8, CONV_W), _F32), pltpu.VMEM((RET_W, RET_W), _F32)]
    return pl.pallas_call(
        _pre_kernel, grid=(1 + batch * NRS,), in_specs=in_specs, out_specs=out_specs, out_shape=out_shape,
        scratch_shapes=scratch, compiler_params=_params(), name="pre")(*args)


def _attn_call(batch, bounded, q, k, vt, k_m, vt_m):
    return pl.pallas_call(
        _attn_kernel, grid=(batch,),
        in_specs=[pl.BlockSpec(memory_space=pltpu.SMEM),
                  pl.BlockSpec((1, SEQ, ATT_W), lambda b: (b, 0, 0)),
                  pl.BlockSpec((1, SEQ, ATT_W), lambda b: (b, 0, 0)),
                  pl.BlockSpec((1, NRT, VT_W, TQ), lambda b: (b, 0, 0, 0)),
                  pl.BlockSpec((1, TQ, ATT_W), lambda b: (0, META_TILE, 0)),
                  pl.BlockSpec((1, 1, VT_W, TQ), lambda b: (0, META_TILE, 0, 0))],
        out_specs=pl.BlockSpec((1, NRT, O_W, TQ), lambda b: (b, 0, 0, 0)),
        out_shape=jax.ShapeDtypeStruct((batch, NRT, O_W, TQ), _BF16),
        scratch_shapes=[pltpu.VMEM((MLA_HEADS, VT_ROWS, TQ), _F32), pltpu.VMEM((MLA_HEADS, 1, TQ), _F32),
                        pltpu.VMEM((MLA_HEADS, TQ, TQ), _F32)],
        compiler_params=_params(), name="attn")(bounded, q, k, vt, k_m, vt_m)


def _attn_meta_call(q_m, k_m, vt_m):
    tile = pl.BlockSpec((1, TQ, ATT_W), lambda s: (0, META_TILE, 0))
    return pl.pallas_call(
        _attn_meta_kernel, grid=(1,),
        in_specs=[tile, tile, pl.BlockSpec((1, 1, VT_W, TQ), lambda s: (0, META_TILE, 0, 0))],
        out_specs=pl.BlockSpec((1, SUB, O_W, TQ), lambda s: (0, 0, 0, 0)),
        out_shape=jax.ShapeDtypeStruct((1, SUB, O_W, TQ), _BF16),
        compiler_params=_params(), name="attn_meta")(q_m, k_m, vt_m)


def _post_call(batch, x, x_meta, ycr, ycr_m, ot, ot_m, w):
    with_meta = x_meta is not None
    consts = [w[n] for n in ("w_out", "g_mlp", "w1", "w2")]
    const_specs = [_const_spec(c.shape) for c in consts]
    if with_meta:
        in_specs = ([_main_step_spec(D_MODEL), _const_spec((STEP, D_MODEL)),
                     _main_step_spec(MIX_HALF), pl.BlockSpec((1, STEP, MIX_HALF), lambda s: (0, 0, 0)),
                     _main_tiles_spec(O_W), pl.BlockSpec((1, SUB, O_W, TQ), lambda s: (0, 0, 0, 0))]
                    + const_specs)
        args = [x, x_meta, ycr, ycr_m, ot, ot_m] + consts
        out_specs = [_main_step_spec(D_MODEL), pl.BlockSpec((1, STEP, D_MODEL), lambda s: (0, 0, 0))]
        out_shape = [jax.ShapeDtypeStruct((batch, SEQ, D_MODEL), _F32),
                     jax.ShapeDtypeStruct((1, STEP, D_MODEL), _F32)]
        grid = (1 + batch * NRS,)
    else:
        in_specs = ([pl.BlockSpec((1, STEP, D_MODEL), lambda s: (s // NRS, s % NRS, 0)),
                     pl.BlockSpec((1, STEP, MIX_HALF), lambda s: (s // NRS, s % NRS, 0)),
                     pl.BlockSpec((1, SUB, O_W, TQ), lambda s: (s // NRS, s % NRS, 0, 0))] + const_specs)
        args = [x, ycr, ot] + consts
        out_specs = pl.BlockSpec((1, STEP, D_MODEL), lambda s: (s // NRS, s % NRS, 0))
        out_shape = jax.ShapeDtypeStruct((batch, SEQ, D_MODEL), _F32)
        grid = (batch * NRS,)
    return pl.pallas_call(
        functools.partial(_post_kernel, with_meta), grid=grid, in_specs=in_specs, out_specs=out_specs,
        out_shape=out_shape, compiler_params=_params(), name="post")(*args)


def kernel(x, meta_tokens, attn_norm_g, w_in, conv_w, ret_gn_g, q_norm_g, w_uq, kv_norm_g, w_ukv,
           q_head_norm_g, k_head_norm_g, w_out, mlp_norm_g, w_mlp_in, w_mlp_out):
    batch = x.shape[0]
    depth = w_in.shape[0]
    assert x.shape[1:] == (SEQ, D_MODEL) and depth >= 1
    t = _tables()
    h = x
    h_meta = jnp.concatenate([jnp.zeros((PAD, D_MODEL), x.dtype), meta_tokens.astype(x.dtype)], axis=0)
    for l in range(depth):
        w = _layer_weights(l, attn_norm_g, w_in, conv_w, ret_gn_g, q_norm_g, w_uq, kv_norm_g, w_ukv,
                           q_head_norm_g, k_head_norm_g, w_out, mlp_norm_g, w_mlp_in, w_mlp_out)
        ycr, q, k, vt, ycr_m, q_m, k_m, vt_m = _pre_call(batch, h, h_meta, w, t)
        ot = _attn_call(batch, w["bounded"], q, k, vt, k_m, vt_m)
        if l == depth - 1:
            h = _post_call(batch, h, None, ycr, None, ot, None, w)
        else:
            ot_m = _attn_meta_call(q_m, k_m, vt_m)
            h, h_meta3 = _post_call(batch, h, h_meta, ycr, ycr_m, ot, ot_m, w)
            h_meta = h_meta3[0]
    return h
```

```python
import functools
import math

import jax
import jax.numpy as jnp
from jax import lax
from jax.experimental import pallas as pl
from jax.experimental.pallas import tpu as pltpu

D_MODEL = 1024
SEQ = 2048
N_META = 16
CONV_W = 256
CONV_K = 3
RET_HEADS = 4
RET_DK = 64
RET_W = RET_HEADS * RET_DK
MLA_HEADS = 8
MLA_NOPE = 64
MLA_ROPE = 32
MLA_V = 64
MLA_QK = MLA_NOPE + MLA_ROPE
Q_LORA = 256
KV_LORA = 128
D_FF = 4 * D_MODEL
ROPE_BASE = 10000.0
NORM_EPS = 1e-6

LANES = 128
BF16_ROWS = 16
HEAD_W = LANES
ATT_W = MLA_HEADS * HEAD_W
VT_ROWS = MLA_V + BF16_ROWS
VT_W = MLA_HEADS * VT_ROWS
MIX_HALF = CONV_W + RET_W
O_W = MLA_HEADS * MLA_V

TQ = 256
SUB = 2
STEP = SUB * TQ
PAD = STEP - N_META
META_TILE = PAD // TQ
NRS = SEQ // STEP
NRT = SEQ // TQ

OFF_CB, OFF_CC, OFF_CH = 0, 256, 512
OFF_RQ, OFF_RK, OFF_RV, OFF_RG = 768, 1024, 1280, 1536
OFF_CQ, OFF_CKV, OFF_KR = 1792, 2048, 2176
D_IN_EXT = OFF_KR + HEAD_W

KV_UNROLL = 4
MASK_VALUE = -1e30
EXP2_SCALE = math.log2(math.e) / math.sqrt(MLA_QK)
MAX_SAFE_BOUND = 50.0
VMEM_LIMIT = 56 * 1024 * 1024

_F32 = jnp.float32
_BF16 = jnp.bfloat16
_NT_DIMS = (((1,), (1,)), ((), ()))
_TN_DIMS = (((0,), (0,)), ((), ()))


def _rms(x, n):
    return x * lax.rsqrt(jnp.sum(x * x, axis=-1, keepdims=True) * (1.0 / n) + NORM_EPS)


def _rot_pairs(z, half, first):
    w = z.shape[-1]
    return jnp.where(first, pltpu.roll(z, w - half, axis=1), pltpu.roll(z, half, axis=1))


def _meta_rows(xm):
    row = lax.broadcasted_iota(jnp.int32, (STEP, 1), 0)
    return jnp.where(row >= PAD, xm, 0.0)


def _pre_kernel(x_ref, xm_ref, *refs):
    consts = refs[:len(_PRE_CONSTS)]
    (ycr_ref, q_ref, k_ref, vt_ref, ycr_m, q_m, k_m, vt_m,
     ubuf, state, halo0, state0) = refs[len(_PRE_CONSTS):]
    s = pl.program_id(0)

    @pl.when(s == 0)
    def _():
        ubuf[0:8, :] = jnp.zeros((8, CONV_W), _F32)
        state[...] = jnp.zeros_like(state)
        x = _meta_rows(xm_ref[...])
        for sub in range(SUB):
            _pre_tile(True, sub, x[sub * TQ:(sub + 1) * TQ], consts, ycr_m, q_m, k_m, vt_m, ubuf, state)
        halo0[...] = ubuf[0:8, :]
        state0[...] = state[...]

    @pl.when(s > 0)
    def _():
        @pl.when((s - 1) % NRS == 0)
        def _():
            ubuf[0:8, :] = halo0[...]
            state[...] = state0[...]

        x = x_ref[0]
        for sub in range(SUB):
            _pre_tile(False, sub, x[sub * TQ:(sub + 1) * TQ], consts, ycr_ref, q_ref, k_ref, vt_ref, ubuf, state)


def _pre_tile(meta, sub, x, consts, ycr_ref, q_ref, k_ref, vt_ref, ubuf, state):
    (g_attn, w_in, conv_w, cosr, sinr, dmat, xi, zeta, cdec, bdmask, gmat, ret_g,
     qn_g, w_uq_t, w_uq_rot_t, kvn_g, w_k, w_vt, gpair, q_cos_t, q_sin_t, gk, cosm, sinm, qk_bound) = consts
    rows = slice(sub * TQ, (sub + 1) * TQ)
    hn = (_rms(x, D_MODEL) * g_attn[...]).astype(_BF16)
    proj_m = jnp.dot(hn, w_in[:, OFF_CQ:D_IN_EXT], preferred_element_type=_F32)
    proj_r = jnp.dot(hn, w_in[:, OFF_RQ:OFF_CQ], preferred_element_type=_F32)

    cq = proj_m[:, 0:Q_LORA]
    ckv = proj_m[:, Q_LORA:Q_LORA + KV_LORA]
    kr128 = proj_m[:, Q_LORA + KV_LORA:]
    cqn = (_rms(cq, Q_LORA) * qn_g[...]).astype(_BF16)
    ckvn = (_rms(ckv, KV_LORA) * kvn_g[...]).astype(_BF16)
    qraw = lax.dot_general(w_uq_t[...], cqn, _NT_DIMS, preferred_element_type=_F32)
    qrot = lax.dot_general(w_uq_rot_t[...], cqn, _NT_DIMS, preferred_element_type=_F32)
    kraw = jnp.dot(ckvn, w_k[...], preferred_element_type=_F32)
    vt = lax.dot_general(w_vt[...], ckvn, _NT_DIMS, preferred_element_type=_F32)
    vrow = lax.broadcasted_iota(jnp.int32, (VT_W, 1), 0)
    vt_ref[0, sub] = jnp.where((vrow % VT_ROWS) >= MLA_V, 1.0, vt).astype(_BF16)

    lane = lax.broadcasted_iota(jnp.int32, (1, RET_W), 1)
    first = (lane % RET_DK) < (RET_DK // 2)
    rq = proj_r[:, 0:RET_W]
    rk = proj_r[:, RET_W:2 * RET_W]
    rv = proj_r[:, 2 * RET_W:3 * RET_W]
    rg = proj_r[:, 3 * RET_W:]
    cr, sr = cosr[rows, :], sinr[rows, :]
    qr = rq * cr + _rot_pairs(rq, RET_DK // 2, first) * sr
    kr = (rk * cr + _rot_pairs(rk, RET_DK // 2, first) * sr) * (RET_DK ** -0.5)
    kb = kr.astype(_BF16)
    hmasks = [(lane // RET_DK) == h for h in range(RET_HEADS)]
    scs = [lax.dot_general(jnp.where(hmasks[h], qr, 0.0).astype(_BF16), kb, _NT_DIMS,
                           preferred_element_type=_F32) for h in range(RET_HEADS)]

    g2 = gpair[...]
    q2 = (qraw * qraw).astype(_BF16)
    k2 = (kraw * kraw).astype(_BF16)
    kr_ms = jnp.dot((kr128 * kr128).astype(_BF16), g2[0:HEAD_W, 0:HEAD_W], preferred_element_type=_F32)
    kr_ms2 = jnp.concatenate([kr_ms, kr_ms], axis=1)
    rq2, rk2 = [], []
    for pr in range(MLA_HEADS // 2):
        psl = slice(2 * pr * HEAD_W, (2 * pr + 2) * HEAD_W)
        rq2.append(lax.rsqrt(jnp.dot(g2, q2[psl, :], preferred_element_type=_F32) + NORM_EPS))
        rk2.append(lax.rsqrt(jnp.dot(k2[:, psl], g2, preferred_element_type=_F32) + kr_ms2 + NORM_EPS))

    proj_c = jnp.dot(hn, w_in[:, OFF_CB:OFF_RQ], preferred_element_type=_F32)

    st = state[...]
    cross = jnp.dot(qr.astype(_BF16), st.astype(_BF16), preferred_element_type=_F32) * xi[...]
    upd = lax.dot_general((kr * zeta[...]).astype(_BF16), rv.astype(_BF16), _TN_DIMS,
                          preferred_element_type=_F32)
    state[...] = cdec[...] * st + bdmask[...] * upd
    inner = jnp.zeros((TQ, RET_W), _F32)
    for h in range(RET_HEADS):
        sc = (scs[h] * dmat[h]).astype(_BF16)
        vh = jnp.where(hmasks[h], rv, 0.0).astype(_BF16)
        inner = inner + jnp.dot(sc, vh, preferred_element_type=_F32)
    o = inner + cross
    msq = jnp.dot((o * o).astype(_BF16), gmat[...], preferred_element_type=_F32)

    lane_h = lax.broadcasted_iota(jnp.int32, (1, HEAD_W), 1)
    first_m = lane_h < (MLA_NOPE + MLA_ROPE // 2)
    cm, sm = cosm[rows, :], sinm[rows, :]
    gkv = gk[...]
    q_cos, q_sin = q_cos_t[:, rows], q_sin_t[:, rows]
    krg = kr128 * gkv
    k_rope = krg * cm + _rot_pairs(krg, MLA_ROPE // 2, first_m) * sm
    one_q = jnp.where(lax.broadcasted_iota(jnp.int32, (HEAD_W, 1), 0) == MLA_QK, 1.0, 0.0)
    kbias = jnp.where(lane_h == MLA_QK, -qk_bound[0], 0.0)
    if meta:
        rowm = sub * TQ + lax.broadcasted_iota(jnp.int32, (TQ, 1), 0)
        kbias = jnp.where((rowm < PAD) & (lane_h == MLA_QK), MASK_VALUE, kbias)
    for h in range(MLA_HEADS):
        sl = slice(h * HEAD_W, (h + 1) * HEAD_W)
        rsl = slice((h % 2) * HEAD_W, (h % 2 + 1) * HEAD_W)
        q_ref[0, sub, sl, :] = (rq2[h // 2][rsl, :] * (qraw[sl, :] * q_cos + qrot[sl, :] * q_sin)
                                + one_q).astype(_BF16)
        k_ref[0, rows, sl] = (rk2[h // 2][:, rsl] * (kraw[:, sl] * gkv + k_rope) + kbias).astype(_BF16)

    cb = proj_c[:, OFF_CB:OFF_CB + CONV_W]
    u = proj_c[:, OFF_CC:OFF_CC + CONV_W] * proj_c[:, OFF_CH:OFF_CH + CONV_W]
    ubuf[8:8 + TQ, :] = u
    cw = conv_w[...]
    conv = cw[0:1] * ubuf[6:6 + TQ, :] + cw[1:2] * ubuf[7:7 + TQ, :] + cw[2:3] * u
    ubuf[0:8, :] = u[TQ - 8:, :]
    ycr_ref[0, rows, 0:CONV_W] = (cb * conv).astype(_BF16)

    o_n = o * lax.rsqrt(msq + NORM_EPS) * ret_g[...]
    gate = rg * (1.0 / (1.0 + jnp.exp(-rg)))
    ycr_ref[0, rows, CONV_W:MIX_HALF] = (gate * o_n).astype(_BF16)


def _attn_kernel(bounded_ref, q_ref, k_ref, vt_ref, km_ref, vtm_ref, ot_ref, acc_scr, m_scr, s_scr):
    bounded = bounded_ref[0] != 0
    run_all = functools.partial(_attn_pipeline, q_ref, k_ref, vt_ref, km_ref, vtm_ref, ot_ref, acc_scr, m_scr,
                                s_scr)
    pl.when(bounded)(functools.partial(run_all, True))
    pl.when(jnp.logical_not(bounded))(functools.partial(run_all, False))


_META_BLOCK = "meta"


def _attn_pipeline(q_ref, k_ref, vt_ref, km_ref, vtm_ref, ot_ref, acc_scr, m_scr, s_scr, bounded):
    causal = (lax.broadcasted_iota(jnp.int32, (TQ, 1), 0) <= lax.broadcasted_iota(jnp.int32, (1, TQ), 1))
    heads = [slice(h * HEAD_W, (h + 1) * HEAD_W) for h in range(MLA_HEADS)]
    vheads = [slice(h * VT_ROWS, (h + 1) * VT_ROWS) for h in range(MLA_HEADS)]

    def tile_start(t):
        return t * TQ if isinstance(t, int) else pl.multiple_of(t * TQ, TQ)

    def scores(i, j, h):
        if j is _META_BLOCK:
            keys = km_ref[0, :, heads[h]]
        else:
            keys = k_ref[0, pl.ds(tile_start(j), TQ), heads[h]]
        return jnp.dot(keys, q_ref[0, i, heads[h], :], preferred_element_type=_F32)

    def stage(j, nxt, diagonal=False):
        for h in range(MLA_HEADS):
            s = s_scr[h]
            s_scr[h] = scores(nxt[0], nxt[1], h)
            if diagonal:
                s = jnp.where(causal, s, MASK_VALUE)
            vth = vtm_ref[0, 0, vheads[h], :] if j is _META_BLOCK else vt_ref[0, j, vheads[h], :]
            if bounded:
                acc_scr[h] = acc_scr[h] + jnp.dot(vth, jnp.exp2(s).astype(_BF16), preferred_element_type=_F32)
                continue
            m_prev = m_scr[h]
            m_new = jnp.maximum(m_prev, jnp.max(s, axis=0, keepdims=True))
            p = jnp.exp2(s - m_new)
            alpha = jnp.exp2(m_prev - m_new)
            pv = jnp.dot(vth, p.astype(_BF16), preferred_element_type=_F32)
            acc_scr[h] = acc_scr[h] * alpha + pv
            m_scr[h] = m_new

    for h in range(MLA_HEADS):
        s_scr[h] = scores(0, 0, h)

    def tile_body(i, carry):
        if not bounded:
            m_scr[...] = jnp.full(m_scr.shape, MASK_VALUE, _F32)
        acc_scr[...] = jnp.zeros(acc_scr.shape, _F32)

        def run(j0, n):
            for u in range(n):
                stage(j0 + u, (i, j0 + u + 1))

        def unrolled_body(jj, c):
            run(jj * KV_UNROLL, KV_UNROLL)
            return c

        def finish(rem):
            run(i - rem, rem)
            stage(i, (i, _META_BLOCK), diagonal=True)
            nxt_tile = jnp.minimum(i + 1, NRT - 1)
            stage(_META_BLOCK, (nxt_tile, 0))
            for h in range(MLA_HEADS):
                a = acc_scr[h]
                inv = 1.0 / a[MLA_V:MLA_V + 8]
                ot_ref[0, i, h * MLA_V:(h + 1) * MLA_V, :] = (
                    a[0:MLA_V] * jnp.concatenate([inv] * (MLA_V // 8), axis=0)).astype(_BF16)

        lax.fori_loop(0, i // KV_UNROLL, unrolled_body, 0)
        for rem in range(KV_UNROLL):
            pl.when(i % KV_UNROLL == rem)(functools.partial(finish, rem))
        return carry

    lax.fori_loop(0, NRT, tile_body, 0)


def _attn_meta_kernel(q_ref, k_ref, vt_ref, ot_ref):
    causal = (lax.broadcasted_iota(jnp.int32, (TQ, 1), 0) <= lax.broadcasted_iota(jnp.int32, (1, TQ), 1))
    for t in range(SUB):
        if t != META_TILE:
            ot_ref[0, t] = jnp.zeros((O_W, TQ), _BF16)
    for h in range(MLA_HEADS):
        sl = slice(h * HEAD_W, (h + 1) * HEAD_W)
        s = jnp.dot(k_ref[0, :, sl], q_ref[0, 0, sl, :], preferred_element_type=_F32)
        s = jnp.where(causal, s, MASK_VALUE)
        p = jnp.exp2(s - jnp.max(s, axis=0, keepdims=True))
        a = jnp.dot(vt_ref[0, 0, h * VT_ROWS:(h + 1) * VT_ROWS, :], p.astype(_BF16), preferred_element_type=_F32)
        inv = 1.0 / a[MLA_V:MLA_V + 8]
        ot_ref[0, META_TILE, h * MLA_V:(h + 1) * MLA_V, :] = (
            a[0:MLA_V] * jnp.concatenate([inv] * (MLA_V // 8), axis=0)).astype(_BF16)


def _post_step(x, ycr_ref, ot_ref, w_out, g_mlp, w1, w2, out_ref):
    tiles = [slice(t * TQ, (t + 1) * TQ) for t in range(SUB)]
    x1 = [x[r] + jnp.dot(ycr_ref[0, r, :], w_out[0:MIX_HALF, :], preferred_element_type=_F32)
          + lax.dot_general(ot_ref[0, t], w_out[MIX_HALF:, :], _TN_DIMS, preferred_element_type=_F32)
          for t, r in enumerate(tiles)]
    up = [jnp.dot((_rms(v, D_MODEL) * g_mlp[...]).astype(_BF16), w1[...], preferred_element_type=_F32)
          for v in x1]
    down = [jnp.dot(jnp.square(jnp.maximum(u, 0.0)).astype(_BF16), w2[...], preferred_element_type=_F32)
            for u in up]
    for t, r in enumerate(tiles):
        out_ref[0, r, :] = x1[t] + down[t]


def _post_kernel(with_meta, *refs):
    if not with_meta:
        x_ref, ycr_ref, ot_ref, w_out, g_mlp, w1, w2, out_ref = refs
        _post_step(x_ref[0], ycr_ref, ot_ref, w_out, g_mlp, w1, w2, out_ref)
        return
    x_ref, xm_ref, ycr_ref, ycr_m, ot_ref, ot_m, w_out, g_mlp, w1, w2, out_ref, out_m = refs
    s = pl.program_id(0)

    @pl.when(s == 0)
    def _():
        _post_step(_meta_rows(xm_ref[...]), ycr_m, ot_m, w_out, g_mlp, w1, w2, out_m)

    @pl.when(s > 0)
    def _():
        _post_step(x_ref[0], ycr_ref, ot_ref, w_out, g_mlp, w1, w2, out_ref)


def _const_spec(shape):
    nd = len(shape)
    return pl.BlockSpec(shape, lambda s, _nd=nd: (0,) * _nd, pipeline_mode=pl.Buffered(1))


def _params():
    return pltpu.CompilerParams(dimension_semantics=("arbitrary",), vmem_limit_bytes=VMEM_LIMIT)


def _batch_step(s):
    r = jnp.maximum(s - 1, 0)
    return r // NRS, r % NRS


def _rope_partner(lh):
    half = MLA_ROPE // 2
    return jnp.where((lh >= MLA_NOPE) & (lh < MLA_NOPE + half), lh + half,
                     jnp.where((lh >= MLA_NOPE + half) & (lh < MLA_QK), lh - half, lh))


def _tables():
    pos = jnp.maximum(jnp.arange(STEP + SEQ, dtype=_F32) - PAD, 0.0)
    inv_r = 1.0 / (ROPE_BASE ** (jnp.arange(0, RET_DK, 2, dtype=_F32) / RET_DK))
    ang_r = pos[:, None] * inv_r[None, :]
    l = jnp.arange(RET_W)
    cosr = jnp.cos(ang_r)[:, l % (RET_DK // 2)]
    sinr = jnp.sin(ang_r)[:, l % (RET_DK // 2)] * jnp.where((l % RET_DK) < RET_DK // 2, -1.0, 1.0)[None, :]
    inv_m = 1.0 / (ROPE_BASE ** (jnp.arange(0, MLA_ROPE, 2, dtype=_F32) / MLA_ROPE))
    ang_m = pos[:, None] * inv_m[None, :]
    lh = jnp.arange(HEAD_W)
    in_rope = (lh >= MLA_NOPE) & (lh < MLA_QK)
    fidx = jnp.clip(lh - MLA_NOPE, 0, MLA_ROPE - 1) % (MLA_ROPE // 2)
    cosm = jnp.where(in_rope[None, :], jnp.cos(ang_m)[:, fidx], 1.0)
    sgn = jnp.where(lh < MLA_NOPE + MLA_ROPE // 2, -1.0, 1.0)
    sinm = jnp.where(in_rope[None, :], jnp.sin(ang_m)[:, fidx] * sgn[None, :], 0.0)
    log_g = jnp.log1p(-jnp.exp2(-5.0 - jnp.arange(RET_HEADS, dtype=_F32)))
    idx = jnp.arange(TQ, dtype=_F32)
    diff = idx[:, None] - idx[None, :]
    dmat = jnp.where(diff >= 0, jnp.exp(jnp.maximum(diff, 0.0)[None] * log_g[:, None, None]), 0.0)
    lg_lane = log_g[l // RET_DK]
    xi = jnp.exp((idx[:, None] + 1.0) * lg_lane[None, :])
    zeta = jnp.exp((TQ - 1.0 - idx[:, None]) * lg_lane[None, :])
    cdec = jnp.exp(TQ * lg_lane)[None, :]
    same = (l[:, None] // RET_DK) == (l[None, :] // RET_DK)
    bdmask = same.astype(_F32)
    gmat = (same.astype(_F32) / RET_DK).astype(_BF16)
    l2 = jnp.arange(2 * HEAD_W)
    gpair = (((l2[:, None] // HEAD_W) == (l2[None, :] // HEAD_W)).astype(_F32) / MLA_QK).astype(_BF16)
    return dict(cosr=cosr, sinr=sinr, cosm=cosm, sinm=sinm, dmat=dmat, xi=xi, zeta=zeta, cdec=cdec,
                bdmask=bdmask, gmat=gmat, gpair=gpair)


def _layer_weights(l, t, attn_norm_g, w_in, conv_w, ret_gn_g, q_norm_g, w_uq, kv_norm_g, w_ukv,
                   q_head_norm_g, k_head_norm_g, w_out, mlp_norm_g, w_mlp_in, w_mlp_out):
    wi = w_in[l]
    w_in_ext = jnp.concatenate(
        [wi[:, :OFF_KR], jnp.zeros((D_MODEL, MLA_NOPE), _F32), wi[:, OFF_KR:],
         jnp.zeros((D_MODEL, HEAD_W - MLA_QK), _F32)], axis=1).astype(_BF16)
    partner = _rope_partner(jnp.arange(HEAD_W))
    is_rope = partner != jnp.arange(HEAD_W)
    wq = jnp.pad(w_uq[l].reshape(Q_LORA, MLA_HEADS, MLA_QK), ((0, 0), (0, 0), (0, HEAD_W - MLA_QK)))
    wq_rot = jnp.where(is_rope[None, None, :], wq[:, :, partner], 0.0)
    wkv = w_ukv[l].reshape(KV_LORA, MLA_HEADS, MLA_NOPE + MLA_V)
    w_k = jnp.pad(wkv[:, :, :MLA_NOPE], ((0, 0), (0, 0), (0, HEAD_W - MLA_NOPE))).reshape(KV_LORA, ATT_W)
    w_vt = jnp.pad(wkv[:, :, MLA_NOPE:], ((0, 0), (0, 0), (0, VT_ROWS - MLA_V))).reshape(KV_LORA, VT_W).T
    gq128 = jnp.pad(q_head_norm_g[l], (0, HEAD_W - MLA_QK)) * EXP2_SCALE
    gk128 = jnp.pad(k_head_norm_g[l], (0, HEAD_W - MLA_QK))
    qk_bound = 1.01 * MLA_QK * jnp.max(jnp.abs(gq128)) * jnp.max(jnp.abs(gk128))
    return dict(
        qk_bound=qk_bound.reshape(1).astype(_F32),
        bounded=(qk_bound <= MAX_SAFE_BOUND).reshape(1).astype(jnp.int32),
        g_attn=attn_norm_g[l][None, :], w_in=w_in_ext, conv_w=conv_w[l], ret_g=ret_gn_g[l].reshape(1, RET_W),
        qn_g=q_norm_g[l][None, :], w_uq_t=wq.reshape(Q_LORA, ATT_W).T.astype(_BF16),
        w_uq_rot_t=wq_rot.reshape(Q_LORA, ATT_W).T.astype(_BF16), kvn_g=kv_norm_g[l][None, :],
        w_k=w_k.astype(_BF16), w_vt=w_vt.astype(_BF16),
        q_cos_t=gq128[:, None] * t["cosm"].T,
        q_sin_t=jnp.where(is_rope, gq128[partner], 0.0)[:, None] * t["sinm"].T, gk=gk128[None, :],
        w_out=w_out[l].astype(_BF16), g_mlp=mlp_norm_g[l][None, :], w1=w_mlp_in[l].astype(_BF16),
        w2=w_mlp_out[l].astype(_BF16))


_PRE_CONSTS = [("g_attn", "w"), ("w_in", "w"), ("conv_w", "w"), ("cosr", "row"), ("sinr", "row"), ("dmat", "t"),
               ("xi", "t"), ("zeta", "t"), ("cdec", "t"), ("bdmask", "t"), ("gmat", "t"), ("ret_g", "w"),
               ("qn_g", "w"), ("w_uq_t", "w"), ("w_uq_rot_t", "w"), ("kvn_g", "w"), ("w_k", "w"), ("w_vt", "w"),
               ("gpair", "t"), ("q_cos_t", "col"), ("q_sin_t", "col"), ("gk", "w"), ("cosm", "row"),
               ("sinm", "row"), ("qk_bound", "smem")]


def _main_step_spec(width):
    def index(s):
        b, i = _batch_step(s)
        return b, i, 0
    return pl.BlockSpec((1, STEP, width), index)


def _main_tiles_spec(rows):
    def index(s):
        b, i = _batch_step(s)
        return b, i, 0, 0
    return pl.BlockSpec((1, SUB, rows, TQ), index)


def _pre_call(batch, x, x_meta, w, t):
    in_specs = [_main_step_spec(D_MODEL), _const_spec((STEP, D_MODEL))]
    args = [x, x_meta]
    for name, kind in _PRE_CONSTS:
        arr = t[name] if kind in ("t", "row") else w[name]
        if kind == "row":
            in_specs.append(pl.BlockSpec((STEP, arr.shape[1]),
                                         lambda s: (jnp.where(s == 0, 0, _batch_step(s)[1] + 1), 0)))
        elif kind == "col":
            in_specs.append(pl.BlockSpec((arr.shape[0], STEP),
                                         lambda s: (0, jnp.where(s == 0, 0, _batch_step(s)[1] + 1))))
        elif kind == "smem":
            in_specs.append(pl.BlockSpec(memory_space=pltpu.SMEM))
        else:
            in_specs.append(_const_spec(arr.shape))
        args.append(arr)
    out_shape = [jax.ShapeDtypeStruct((batch, SEQ, MIX_HALF), _BF16),
                 jax.ShapeDtypeStruct((batch, NRT, ATT_W, TQ), _BF16),
                 jax.ShapeDtypeStruct((batch, SEQ, ATT_W), _BF16),
                 jax.ShapeDtypeStruct((batch, NRT, VT_W, TQ), _BF16),
                 jax.ShapeDtypeStruct((1, STEP, MIX_HALF), _BF16),
                 jax.ShapeDtypeStruct((1, SUB, ATT_W, TQ), _BF16),
                 jax.ShapeDtypeStruct((1, STEP, ATT_W), _BF16),
                 jax.ShapeDtypeStruct((1, SUB, VT_W, TQ), _BF16)]
    out_specs = [_main_step_spec(MIX_HALF), _main_tiles_spec(ATT_W), _main_step_spec(ATT_W),
                 _main_tiles_spec(VT_W),
                 pl.BlockSpec((1, STEP, MIX_HALF), lambda s: (0, 0, 0)),
                 pl.BlockSpec((1, SUB, ATT_W, TQ), lambda s: (0, 0, 0, 0)),
                 pl.BlockSpec((1, STEP, ATT_W), lambda s: (0, 0, 0)),
                 pl.BlockSpec((1, SUB, VT_W, TQ), lambda s: (0, 0, 0, 0))]
    scratch = [pltpu.VMEM((TQ + 8, CONV_W), _F32), pltpu.VMEM((RET_W, RET_W), _F32),
               pltpu.VMEM((8, CONV_W), _F32), pltpu.VMEM((RET_W, RET_W), _F32)]
    return pl.pallas_call(
        _pre_kernel, grid=(1 + batch * NRS,), in_specs=in_specs, out_specs=out_specs, out_shape=out_shape,
        scratch_shapes=scratch, compiler_params=_params(), name="pre")(*args)


def _attn_call(batch, bounded, q, k, vt, k_m, vt_m):
    return pl.pallas_call(
        _attn_kernel, grid=(batch,),
        in_specs=[pl.BlockSpec(memory_space=pltpu.SMEM),
                  pl.BlockSpec((1, NRT, ATT_W, TQ), lambda b: (b, 0, 0, 0)),
                  pl.BlockSpec((1, SEQ, ATT_W), lambda b: (b, 0, 0)),
                  pl.BlockSpec((1, NRT, VT_W, TQ), lambda b: (b, 0, 0, 0)),
                  pl.BlockSpec((1, TQ, ATT_W), lambda b: (0, META_TILE, 0)),
                  pl.BlockSpec((1, 1, VT_W, TQ), lambda b: (0, META_TILE, 0, 0))],
        out_specs=pl.BlockSpec((1, NRT, O_W, TQ), lambda b: (b, 0, 0, 0)),
        out_shape=jax.ShapeDtypeStruct((batch, NRT, O_W, TQ), _BF16),
        scratch_shapes=[pltpu.VMEM((MLA_HEADS, VT_ROWS, TQ), _F32), pltpu.VMEM((MLA_HEADS, 1, TQ), _F32),
                        pltpu.VMEM((MLA_HEADS, TQ, TQ), _F32)],
        compiler_params=_params(), name="attn")(bounded, q, k, vt, k_m, vt_m)


def _attn_meta_call(q_m, k_m, vt_m):
    return pl.pallas_call(
        _attn_meta_kernel, grid=(1,),
        in_specs=[pl.BlockSpec((1, 1, ATT_W, TQ), lambda s: (0, META_TILE, 0, 0)),
                  pl.BlockSpec((1, TQ, ATT_W), lambda s: (0, META_TILE, 0)),
                  pl.BlockSpec((1, 1, VT_W, TQ), lambda s: (0, META_TILE, 0, 0))],
        out_specs=pl.BlockSpec((1, SUB, O_W, TQ), lambda s: (0, 0, 0, 0)),
        out_shape=jax.ShapeDtypeStruct((1, SUB, O_W, TQ), _BF16),
        compiler_params=_params(), name="attn_meta")(q_m, k_m, vt_m)


def _post_call(batch, x, x_meta, ycr, ycr_m, ot, ot_m, w):
    with_meta = x_meta is not None
    consts = [w[n] for n in ("w_out", "g_mlp", "w1", "w2")]
    const_specs = [_const_spec(c.shape) for c in consts]
    if with_meta:
        in_specs = ([_main_step_spec(D_MODEL), _const_spec((STEP, D_MODEL)),
                     _main_step_spec(MIX_HALF), pl.BlockSpec((1, STEP, MIX_HALF), lambda s: (0, 0, 0)),
                     _main_tiles_spec(O_W), pl.BlockSpec((1, SUB, O_W, TQ), lambda s: (0, 0, 0, 0))]
                    + const_specs)
        args = [x, x_meta, ycr, ycr_m, ot, ot_m] + consts
        out_specs = [_main_step_spec(D_MODEL), pl.BlockSpec((1, STEP, D_MODEL), lambda s: (0, 0, 0))]
        out_shape = [jax.ShapeDtypeStruct((batch, SEQ, D_MODEL), _F32),
                     jax.ShapeDtypeStruct((1, STEP, D_MODEL), _F32)]
        grid = (1 + batch * NRS,)
    else:
        in_specs = ([pl.BlockSpec((1, STEP, D_MODEL), lambda s: (s // NRS, s % NRS, 0)),
                     pl.BlockSpec((1, STEP, MIX_HALF), lambda s: (s // NRS, s % NRS, 0)),
                     pl.BlockSpec((1, SUB, O_W, TQ), lambda s: (s // NRS, s % NRS, 0, 0))] + const_specs)
        args = [x, ycr, ot] + consts
        out_specs = pl.BlockSpec((1, STEP, D_MODEL), lambda s: (s // NRS, s % NRS, 0))
        out_shape = jax.ShapeDtypeStruct((batch, SEQ, D_MODEL), _F32)
        grid = (batch * NRS,)
    return pl.pallas_call(
        functools.partial(_post_kernel, with_meta), grid=grid, in_specs=in_specs, out_specs=out_specs,
        out_shape=out_shape, compiler_params=_params(), name="post")(*args)


def kernel(x, meta_tokens, attn_norm_g, w_in, conv_w, ret_gn_g, q_norm_g, w_uq, kv_norm_g, w_ukv,
           q_head_norm_g, k_head_norm_g, w_out, mlp_norm_g, w_mlp_in, w_mlp_out):
    batch = x.shape[0]
    depth = w_in.shape[0]
    assert x.shape[1:] == (SEQ, D_MODEL) and depth >= 1
    t = _tables()
    h = x
    h_meta = jnp.concatenate([jnp.zeros((PAD, D_MODEL), x.dtype), meta_tokens.astype(x.dtype)], axis=0)
    for l in range(depth):
        w = _layer_weights(l, t, attn_norm_g, w_in, conv_w, ret_gn_g, q_norm_g, w_uq, kv_norm_g, w_ukv,
                           q_head_norm_g, k_head_norm_g, w_out, mlp_norm_g, w_mlp_in, w_mlp_out)
        ycr, q, k, vt, ycr_m, q_m, k_m, vt_m = _pre_call(batch, h, h_meta, w, t)
        ot = _attn_call(batch, w["bounded"], q, k, vt, k_m, vt_m)
        if l == depth - 1:
            h = _post_call(batch, h, None, ycr, None, ot, None, w)
        else:
            ot_m = _attn_meta_call(q_m, k_m, vt_m)
            h, h_meta3 = _post_call(batch, h, h_meta, ycr, ycr_m, ot, ot_m, w)
            h_meta = h_meta3[0]
    return h
```

```python
import functools
import math

import jax
import jax.numpy as jnp
import numpy as np
from jax import lax
from jax.experimental import pallas as pl
from jax.experimental.pallas import tpu as pltpu

D_MODEL = 1024
SEQ = 2048
N_META = 16
CONV_W = 256
CONV_K = 3
RET_HEADS = 4
RET_DK = 64
RET_W = RET_HEADS * RET_DK
MLA_HEADS = 8
MLA_NOPE = 64
MLA_ROPE = 32
MLA_V = 64
MLA_QK = MLA_NOPE + MLA_ROPE
Q_LORA = 256
KV_LORA = 128
D_FF = 4 * D_MODEL
ROPE_BASE = 10000.0
NORM_EPS = 1e-6

LANES = 128
BF16_ROWS = 16
HEAD_W = LANES
ATT_W = MLA_HEADS * HEAD_W
VT_ROWS = MLA_V + BF16_ROWS
VT_W = MLA_HEADS * VT_ROWS
MIX_HALF = CONV_W + RET_W
O_W = MLA_HEADS * MLA_V

TQ = 256
SUB = 2
STEP = SUB * TQ
PAD = STEP - N_META
META_TILE = PAD // TQ
NRS = SEQ // STEP
NRT = SEQ // TQ

OFF_CB, OFF_CC, OFF_CH = 0, 256, 512
OFF_RQ, OFF_RK, OFF_RV, OFF_RG = 768, 1024, 1280, 1536
OFF_CQ, OFF_CKV, OFF_KR = 1792, 2048, 2176
D_IN_EXT = OFF_KR + HEAD_W

KV_UNROLL = 4
MASK_VALUE = -1e30
EXP2_SCALE = math.log2(math.e) / math.sqrt(MLA_QK)
MAX_SAFE_BOUND = 50.0
VMEM_LIMIT = 56 * 1024 * 1024

_F32 = jnp.float32
_BF16 = jnp.bfloat16
_NT_DIMS = (((1,), (1,)), ((), ()))
_TN_DIMS = (((0,), (0,)), ((), ()))


def _rms(x, n):
    return x * lax.rsqrt(jnp.sum(x * x, axis=-1, keepdims=True) * (1.0 / n) + NORM_EPS)


def _rot_pairs(z, half, first):
    w = z.shape[-1]
    return jnp.where(first, pltpu.roll(z, w - half, axis=1), pltpu.roll(z, half, axis=1))


def _meta_rows(xm):
    row = lax.broadcasted_iota(jnp.int32, (STEP, 1), 0)
    return jnp.where(row >= PAD, xm, 0.0)


def _pre_kernel(x_ref, xm_ref, *refs):
    consts = refs[:len(_PRE_CONSTS)]
    (ycr_ref, q_ref, k_ref, vt_ref, ycr_m, q_m, k_m, vt_m,
     ubuf, state, halo0, state0) = refs[len(_PRE_CONSTS):]
    s = pl.program_id(0)

    @pl.when(s == 0)
    def _():
        ubuf[0:8, :] = jnp.zeros((8, CONV_W), _F32)
        state[...] = jnp.zeros_like(state)
        x = _meta_rows(xm_ref[...])
        for sub in range(SUB):
            _pre_tile(True, sub, x[sub * TQ:(sub + 1) * TQ], consts, ycr_m, q_m, k_m, vt_m, ubuf, state)
        halo0[...] = ubuf[0:8, :]
        state0[...] = state[...]

    @pl.when(s > 0)
    def _():
        @pl.when((s - 1) % NRS == 0)
        def _():
            ubuf[0:8, :] = halo0[...]
            state[...] = state0[...]

        x = x_ref[0]
        for sub in range(SUB):
            _pre_tile(False, sub, x[sub * TQ:(sub + 1) * TQ], consts, ycr_ref, q_ref, k_ref, vt_ref, ubuf, state)


def _pre_tile(meta, sub, x, consts, ycr_ref, q_ref, k_ref, vt_ref, ubuf, state):
    (g_attn, w_in, conv_w, cosr, sinr, dmat, xi, zeta, cdec, bdmask, gmat, ret_g,
     qn_g, w_uq_t, w_uq_rot_t, kvn_g, w_k, w_vt, gpair, q_cos_t, q_sin_t, gk, cosm, sinm, qk_bound) = consts
    rows = slice(sub * TQ, (sub + 1) * TQ)
    hn = (_rms(x, D_MODEL) * g_attn[...]).astype(_BF16)
    proj_m = jnp.dot(hn, w_in[:, OFF_CQ:D_IN_EXT], preferred_element_type=_F32)
    proj_r = jnp.dot(hn, w_in[:, OFF_RQ:OFF_CQ], preferred_element_type=_F32)

    cq = proj_m[:, 0:Q_LORA]
    ckv = proj_m[:, Q_LORA:Q_LORA + KV_LORA]
    kr128 = proj_m[:, Q_LORA + KV_LORA:]
    cqn = (_rms(cq, Q_LORA) * qn_g[...]).astype(_BF16)
    ckvn = (_rms(ckv, KV_LORA) * kvn_g[...]).astype(_BF16)
    qraw = lax.dot_general(w_uq_t[...], cqn, _NT_DIMS, preferred_element_type=_F32)
    qrot = lax.dot_general(w_uq_rot_t[...], cqn, _NT_DIMS, preferred_element_type=_F32)
    kraw = jnp.dot(ckvn, w_k[...], preferred_element_type=_F32)
    vt = lax.dot_general(w_vt[...], ckvn, _NT_DIMS, preferred_element_type=_F32)
    vrow = lax.broadcasted_iota(jnp.int32, (VT_W, 1), 0)
    vt_ref[0, sub] = jnp.where((vrow % VT_ROWS) >= MLA_V, 1.0, vt).astype(_BF16)

    lane = lax.broadcasted_iota(jnp.int32, (1, RET_W), 1)
    first = (lane % RET_DK) < (RET_DK // 2)
    rq = proj_r[:, 0:RET_W]
    rk = proj_r[:, RET_W:2 * RET_W]
    rv = proj_r[:, 2 * RET_W:3 * RET_W]
    rg = proj_r[:, 3 * RET_W:]
    cr, sr = cosr[rows, :], sinr[rows, :]
    qr = rq * cr + _rot_pairs(rq, RET_DK // 2, first) * sr
    kr = (rk * cr + _rot_pairs(rk, RET_DK // 2, first) * sr) * (RET_DK ** -0.5)
    kb = kr.astype(_BF16)
    hmasks = [(lane // RET_DK) == h for h in range(RET_HEADS)]
    scs = [lax.dot_general(jnp.where(hmasks[h], qr, 0.0).astype(_BF16), kb, _NT_DIMS,
                           preferred_element_type=_F32) for h in range(RET_HEADS)]

    g2 = gpair[...]
    q2 = (qraw * qraw).astype(_BF16)
    k2 = (kraw * kraw).astype(_BF16)
    kr_ms = jnp.dot((kr128 * kr128).astype(_BF16), g2[0:HEAD_W, 0:HEAD_W], preferred_element_type=_F32)
    kr_ms2 = jnp.concatenate([kr_ms, kr_ms], axis=1)
    rq2, rk2 = [], []
    for pr in range(MLA_HEADS // 2):
        psl = slice(2 * pr * HEAD_W, (2 * pr + 2) * HEAD_W)
        q_sq = jnp.dot(g2, q2[psl, :], preferred_element_type=_F32)
        k_sq = jnp.dot(k2[:, psl], g2, preferred_element_type=_F32) + kr_ms2
        rq2.append(lax.rsqrt(q_sq * (1.0 / MLA_QK) + NORM_EPS))
        rk2.append(lax.rsqrt(k_sq * (1.0 / MLA_QK) + NORM_EPS))

    proj_c = jnp.dot(hn, w_in[:, OFF_CB:OFF_RQ], preferred_element_type=_F32)

    st = state[...]
    cross = jnp.dot(qr.astype(_BF16), st.astype(_BF16), preferred_element_type=_F32) * xi[...]
    upd = lax.dot_general((kr * zeta[...]).astype(_BF16), rv.astype(_BF16), _TN_DIMS,
                          preferred_element_type=_F32)
    state[...] = cdec[...] * st + bdmask[...] * upd
    inner = jnp.zeros((TQ, RET_W), _F32)
    for h in range(RET_HEADS):
        sc = (scs[h] * dmat[h]).astype(_BF16)
        vh = jnp.where(hmasks[h], rv, 0.0).astype(_BF16)
        inner = inner + jnp.dot(sc, vh, preferred_element_type=_F32)
    o = inner + cross
    msq = jnp.dot((o * o).astype(_BF16), gmat[...], preferred_element_type=_F32)

    lane_h = lax.broadcasted_iota(jnp.int32, (1, HEAD_W), 1)
    first_m = lane_h < (MLA_NOPE + MLA_ROPE // 2)
    cm, sm = cosm[rows, :], sinm[rows, :]
    gkv = gk[...]
    q_cos, q_sin = q_cos_t[:, rows], q_sin_t[:, rows]
    krg = kr128 * gkv
    k_rope = krg * cm + _rot_pairs(krg, MLA_ROPE // 2, first_m) * sm
    one_q = jnp.where(lax.broadcasted_iota(jnp.int32, (HEAD_W, 1), 0) == MLA_QK, 1.0, 0.0)
    kbias = jnp.where(lane_h == MLA_QK, -qk_bound[0], 0.0)
    if meta:
        rowm = sub * TQ + lax.broadcasted_iota(jnp.int32, (TQ, 1), 0)
        kbias = jnp.where((rowm < PAD) & (lane_h == MLA_QK), MASK_VALUE, kbias)
    for h in range(MLA_HEADS):
        sl = slice(h * HEAD_W, (h + 1) * HEAD_W)
        rsl = slice((h % 2) * HEAD_W, (h % 2 + 1) * HEAD_W)
        q_ref[0, sub, sl, :] = (rq2[h // 2][rsl, :] * (qraw[sl, :] * q_cos + qrot[sl, :] * q_sin)
                                + one_q).astype(_BF16)
        k_ref[0, rows, sl] = (rk2[h // 2][:, rsl] * (kraw[:, sl] * gkv + k_rope) + kbias).astype(_BF16)

    cb = proj_c[:, OFF_CB:OFF_CB + CONV_W]
    u = proj_c[:, OFF_CC:OFF_CC + CONV_W] * proj_c[:, OFF_CH:OFF_CH + CONV_W]
    ubuf[8:8 + TQ, :] = u
    cw = conv_w[...]
    conv = cw[0:1] * ubuf[6:6 + TQ, :] + cw[1:2] * ubuf[7:7 + TQ, :] + cw[2:3] * u
    ubuf[0:8, :] = u[TQ - 8:, :]
    ycr_ref[0, rows, 0:CONV_W] = (cb * conv).astype(_BF16)

    o_n = o * lax.rsqrt(msq + NORM_EPS) * ret_g[...]
    gate = rg * (1.0 / (1.0 + jnp.exp(-rg)))
    ycr_ref[0, rows, CONV_W:MIX_HALF] = (gate * o_n).astype(_BF16)


def _attn_kernel(bounded_ref, q_ref, k_ref, vt_ref, km_ref, vtm_ref, ot_ref, acc_scr, m_scr, s_scr):
    bounded = bounded_ref[0] != 0
    run_all = functools.partial(_attn_pipeline, q_ref, k_ref, vt_ref, km_ref, vtm_ref, ot_ref, acc_scr, m_scr,
                                s_scr)
    pl.when(bounded)(functools.partial(run_all, True))
    pl.when(jnp.logical_not(bounded))(functools.partial(run_all, False))


_META_BLOCK = "meta"


def _attn_pipeline(q_ref, k_ref, vt_ref, km_ref, vtm_ref, ot_ref, acc_scr, m_scr, s_scr, bounded):
    causal = (lax.broadcasted_iota(jnp.int32, (TQ, 1), 0) <= lax.broadcasted_iota(jnp.int32, (1, TQ), 1))
    heads = [slice(h * HEAD_W, (h + 1) * HEAD_W) for h in range(MLA_HEADS)]
    vheads = [slice(h * VT_ROWS, (h + 1) * VT_ROWS) for h in range(MLA_HEADS)]

    def tile_start(t):
        return t * TQ if isinstance(t, int) else pl.multiple_of(t * TQ, TQ)

    def scores(i, j, h):
        if j is _META_BLOCK:
            keys = km_ref[0, :, heads[h]]
        else:
            keys = k_ref[0, pl.ds(tile_start(j), TQ), heads[h]]
        return jnp.dot(keys, q_ref[0, i, heads[h], :], preferred_element_type=_F32)

    def stage(j, nxt, diagonal=False):
        for h in range(MLA_HEADS):
            s = s_scr[h]
            s_scr[h] = scores(nxt[0], nxt[1], h)
            if diagonal:
                s = jnp.where(causal, s, MASK_VALUE)
            vth = vtm_ref[0, 0, vheads[h], :] if j is _META_BLOCK else vt_ref[0, j, vheads[h], :]
            if bounded:
                acc_scr[h] = acc_scr[h] + jnp.dot(vth, jnp.exp2(s).astype(_BF16), preferred_element_type=_F32)
                continue
            m_prev = m_scr[h]
            m_new = jnp.maximum(m_prev, jnp.max(s, axis=0, keepdims=True))
            p = jnp.exp2(s - m_new)
            alpha = jnp.exp2(m_prev - m_new)
            pv = jnp.dot(vth, p.astype(_BF16), preferred_element_type=_F32)
            acc_scr[h] = acc_scr[h] * alpha + pv
            m_scr[h] = m_new

    for h in range(MLA_HEADS):
        s_scr[h] = scores(0, 0, h)

    def tile_body(i, carry):
        if not bounded:
            m_scr[...] = jnp.full(m_scr.shape, MASK_VALUE, _F32)
        acc_scr[...] = jnp.zeros(acc_scr.shape, _F32)

        def run(j0, n):
            for u in range(n):
                stage(j0 + u, (i, j0 + u + 1))

        def unrolled_body(jj, c):
            run(jj * KV_UNROLL, KV_UNROLL)
            return c

        def finish(rem):
            run(i - rem, rem)
            stage(i, (i, _META_BLOCK), diagonal=True)
            nxt_tile = jnp.minimum(i + 1, NRT - 1)
            stage(_META_BLOCK, (nxt_tile, 0))
            for h in range(MLA_HEADS):
                a = acc_scr[h]
                inv = 1.0 / a[MLA_V:MLA_V + 8]
                ot_ref[0, i, h * MLA_V:(h + 1) * MLA_V, :] = (
                    a[0:MLA_V] * jnp.concatenate([inv] * (MLA_V // 8), axis=0)).astype(_BF16)

        lax.fori_loop(0, i // KV_UNROLL, unrolled_body, 0)
        for rem in range(KV_UNROLL):
            pl.when(i % KV_UNROLL == rem)(functools.partial(finish, rem))
        return carry

    lax.fori_loop(0, NRT, tile_body, 0)


def _attn_meta_kernel(q_ref, k_ref, vt_ref, ot_ref):
    causal = (lax.broadcasted_iota(jnp.int32, (TQ, 1), 0) <= lax.broadcasted_iota(jnp.int32, (1, TQ), 1))
    for t in range(SUB):
        if t != META_TILE:
            ot_ref[0, t] = jnp.zeros((O_W, TQ), _BF16)
    for h in range(MLA_HEADS):
        sl = slice(h * HEAD_W, (h + 1) * HEAD_W)
        s = jnp.dot(k_ref[0, :, sl], q_ref[0, 0, sl, :], preferred_element_type=_F32)
        s = jnp.where(causal, s, MASK_VALUE)
        p = jnp.exp2(s - jnp.max(s, axis=0, keepdims=True))
        a = jnp.dot(vt_ref[0, 0, h * VT_ROWS:(h + 1) * VT_ROWS, :], p.astype(_BF16), preferred_element_type=_F32)
        inv = 1.0 / a[MLA_V:MLA_V + 8]
        ot_ref[0, META_TILE, h * MLA_V:(h + 1) * MLA_V, :] = (
            a[0:MLA_V] * jnp.concatenate([inv] * (MLA_V // 8), axis=0)).astype(_BF16)


def _post_step(x, ycr_ref, ot_ref, w_out, g_mlp, w1, w2, out_ref):
    tiles = [slice(t * TQ, (t + 1) * TQ) for t in range(SUB)]
    x1 = [x[r] + jnp.dot(ycr_ref[0, r, :], w_out[0:MIX_HALF, :], preferred_element_type=_F32)
          + lax.dot_general(ot_ref[0, t], w_out[MIX_HALF:, :], _TN_DIMS, preferred_element_type=_F32)
          for t, r in enumerate(tiles)]
    up = [jnp.dot((_rms(v, D_MODEL) * g_mlp[...]).astype(_BF16), w1[...], preferred_element_type=_F32)
          for v in x1]
    down = [jnp.dot(jnp.square(jnp.maximum(u, 0.0)).astype(_BF16), w2[...], preferred_element_type=_F32)
            for u in up]
    for t, r in enumerate(tiles):
        out_ref[0, r, :] = x1[t] + down[t]


def _post_kernel(with_meta, *refs):
    if not with_meta:
        x_ref, ycr_ref, ot_ref, w_out, g_mlp, w1, w2, out_ref = refs
        _post_step(x_ref[0], ycr_ref, ot_ref, w_out, g_mlp, w1, w2, out_ref)
        return
    x_ref, xm_ref, ycr_ref, ycr_m, ot_ref, ot_m, w_out, g_mlp, w1, w2, out_ref, out_m = refs
    s = pl.program_id(0)

    @pl.when(s == 0)
    def _():
        _post_step(_meta_rows(xm_ref[...]), ycr_m, ot_m, w_out, g_mlp, w1, w2, out_m)

    @pl.when(s > 0)
    def _():
        _post_step(x_ref[0], ycr_ref, ot_ref, w_out, g_mlp, w1, w2, out_ref)


def _const_spec(shape):
    nd = len(shape)
    return pl.BlockSpec(shape, lambda s, _nd=nd: (0,) * _nd, pipeline_mode=pl.Buffered(1))


def _params():
    return pltpu.CompilerParams(dimension_semantics=("arbitrary",), vmem_limit_bytes=VMEM_LIMIT)


def _batch_step(s):
    r = jnp.maximum(s - 1, 0)
    return r // NRS, r % NRS


def _rope_partner(lh):
    half = MLA_ROPE // 2
    return jnp.where((lh >= MLA_NOPE) & (lh < MLA_NOPE + half), lh + half,
                     jnp.where((lh >= MLA_NOPE + half) & (lh < MLA_QK), lh - half, lh))


def _tables():
    pos = np.maximum(np.arange(STEP + SEQ, dtype=np.float64) - PAD, 0.0)
    inv_r = 1.0 / (ROPE_BASE ** (np.arange(0, RET_DK, 2, dtype=np.float64) / RET_DK))
    ang_r = pos[:, None] * inv_r[None, :]
    l = np.arange(RET_W)
    cosr = np.cos(ang_r)[:, l % (RET_DK // 2)]
    sinr = np.sin(ang_r)[:, l % (RET_DK // 2)] * np.where((l % RET_DK) < RET_DK // 2, -1.0, 1.0)[None, :]
    inv_m = 1.0 / (ROPE_BASE ** (np.arange(0, MLA_ROPE, 2, dtype=np.float64) / MLA_ROPE))
    ang_m = pos[:, None] * inv_m[None, :]
    lh = np.arange(HEAD_W)
    in_rope = (lh >= MLA_NOPE) & (lh < MLA_QK)
    fidx = np.clip(lh - MLA_NOPE, 0, MLA_ROPE - 1) % (MLA_ROPE // 2)
    cosm = np.where(in_rope[None, :], np.cos(ang_m)[:, fidx], 1.0)
    sgn = np.where(lh < MLA_NOPE + MLA_ROPE // 2, -1.0, 1.0)
    sinm = np.where(in_rope[None, :], np.sin(ang_m)[:, fidx] * sgn[None, :], 0.0)
    log_g = np.log1p(-np.exp2(-5.0 - np.arange(RET_HEADS, dtype=np.float64)))
    idx = np.arange(TQ, dtype=np.float64)
    diff = idx[:, None] - idx[None, :]
    dmat = np.where(diff >= 0, np.exp(np.maximum(diff, 0.0)[None] * log_g[:, None, None]), 0.0)
    lg_lane = log_g[l // RET_DK]
    xi = np.exp((idx[:, None] + 1.0) * lg_lane[None, :])
    zeta = np.exp((TQ - 1.0 - idx[:, None]) * lg_lane[None, :])
    cdec = np.exp(TQ * lg_lane)[None, :]
    same = ((l[:, None] // RET_DK) == (l[None, :] // RET_DK)).astype(np.float64)
    l2 = np.arange(2 * HEAD_W)
    pair = ((l2[:, None] // HEAD_W) == (l2[None, :] // HEAD_W)).astype(np.float64)
    f32 = dict(cosr=cosr, sinr=sinr, cosm=cosm, sinm=sinm, dmat=dmat, xi=xi, zeta=zeta, cdec=cdec, bdmask=same)
    t = {name: np.asarray(v, dtype=np.float32) for name, v in f32.items()}
    t["gmat"] = jnp.asarray((same / RET_DK).astype(np.float32), dtype=_BF16)
    t["gpair"] = jnp.asarray(pair.astype(np.float32), dtype=_BF16)
    return t


def _layer_weights(l, t, attn_norm_g, w_in, conv_w, ret_gn_g, q_norm_g, w_uq, kv_norm_g, w_ukv,
                   q_head_norm_g, k_head_norm_g, w_out, mlp_norm_g, w_mlp_in, w_mlp_out):
    wi = w_in[l]
    w_in_ext = jnp.concatenate(
        [wi[:, :OFF_KR], jnp.zeros((D_MODEL, MLA_NOPE), _F32), wi[:, OFF_KR:],
         jnp.zeros((D_MODEL, HEAD_W - MLA_QK), _F32)], axis=1).astype(_BF16)
    partner = _rope_partner(jnp.arange(HEAD_W))
    is_rope = partner != jnp.arange(HEAD_W)
    wq = jnp.pad(w_uq[l].reshape(Q_LORA, MLA_HEADS, MLA_QK), ((0, 0), (0, 0), (0, HEAD_W - MLA_QK)))
    wq_rot = jnp.where(is_rope[None, None, :], wq[:, :, partner], 0.0)
    wkv = w_ukv[l].reshape(KV_LORA, MLA_HEADS, MLA_NOPE + MLA_V)
    w_k = jnp.pad(wkv[:, :, :MLA_NOPE], ((0, 0), (0, 0), (0, HEAD_W - MLA_NOPE))).reshape(KV_LORA, ATT_W)
    w_vt = jnp.pad(wkv[:, :, MLA_NOPE:], ((0, 0), (0, 0), (0, VT_ROWS - MLA_V))).reshape(KV_LORA, VT_W).T
    gq128 = jnp.pad(q_head_norm_g[l], (0, HEAD_W - MLA_QK)) * EXP2_SCALE
    gk128 = jnp.pad(k_head_norm_g[l], (0, HEAD_W - MLA_QK))
    qk_bound = 1.01 * MLA_QK * jnp.max(jnp.abs(gq128)) * jnp.max(jnp.abs(gk128))
    return dict(
        qk_bound=qk_bound.reshape(1).astype(_F32),
        bounded=(qk_bound <= MAX_SAFE_BOUND).reshape(1).astype(jnp.int32),
        g_attn=attn_norm_g[l][None, :], w_in=w_in_ext, conv_w=conv_w[l], ret_g=ret_gn_g[l].reshape(1, RET_W),
        qn_g=q_norm_g[l][None, :], w_uq_t=wq.reshape(Q_LORA, ATT_W).T.astype(_BF16),
        w_uq_rot_t=wq_rot.reshape(Q_LORA, ATT_W).T.astype(_BF16), kvn_g=kv_norm_g[l][None, :],
        w_k=w_k.astype(_BF16), w_vt=w_vt.astype(_BF16),
        q_cos_t=gq128[:, None] * t["cosm"].T,
        q_sin_t=jnp.where(is_rope, gq128[partner], 0.0)[:, None] * t["sinm"].T, gk=gk128[None, :],
        w_out=w_out[l].astype(_BF16), g_mlp=mlp_norm_g[l][None, :], w1=w_mlp_in[l].astype(_BF16),
        w2=w_mlp_out[l].astype(_BF16))


_PRE_CONSTS = [("g_attn", "w"), ("w_in", "w"), ("conv_w", "w"), ("cosr", "row"), ("sinr", "row"), ("dmat", "t"),
               ("xi", "t"), ("zeta", "t"), ("cdec", "t"), ("bdmask", "t"), ("gmat", "t"), ("ret_g", "w"),
               ("qn_g", "w"), ("w_uq_t", "w"), ("w_uq_rot_t", "w"), ("kvn_g", "w"), ("w_k", "w"), ("w_vt", "w"),
               ("gpair", "t"), ("q_cos_t", "col"), ("q_sin_t", "col"), ("gk", "w"), ("cosm", "row"),
               ("sinm", "row"), ("qk_bound", "smem")]


def _main_step_spec(width):
    def index(s):
        b, i = _batch_step(s)
        return b, i, 0
    return pl.BlockSpec((1, STEP, width), index)


def _main_tiles_spec(rows):
    def index(s):
        b, i = _batch_step(s)
        return b, i, 0, 0
    return pl.BlockSpec((1, SUB, rows, TQ), index)


def _pre_call(batch, x, x_meta, w, t):
    in_specs = [_main_step_spec(D_MODEL), _const_spec((STEP, D_MODEL))]
    args = [x, x_meta]
    for name, kind in _PRE_CONSTS:
        arr = t[name] if kind in ("t", "row") else w[name]
        if kind == "row":
            in_specs.append(pl.BlockSpec((STEP, arr.shape[1]),
                                         lambda s: (jnp.where(s == 0, 0, _batch_step(s)[1] + 1), 0)))
        elif kind == "col":
            in_specs.append(pl.BlockSpec((arr.shape[0], STEP),
                                         lambda s: (0, jnp.where(s == 0, 0, _batch_step(s)[1] + 1))))
        elif kind == "smem":
            in_specs.append(pl.BlockSpec(memory_space=pltpu.SMEM))
        else:
            in_specs.append(_const_spec(arr.shape))
        args.append(arr)
    out_shape = [jax.ShapeDtypeStruct((batch, SEQ, MIX_HALF), _BF16),
                 jax.ShapeDtypeStruct((batch, NRT, ATT_W, TQ), _BF16),
                 jax.ShapeDtypeStruct((batch, SEQ, ATT_W), _BF16),
                 jax.ShapeDtypeStruct((batch, NRT, VT_W, TQ), _BF16),
                 jax.ShapeDtypeStruct((1, STEP, MIX_HALF), _BF16),
                 jax.ShapeDtypeStruct((1, SUB, ATT_W, TQ), _BF16),
                 jax.ShapeDtypeStruct((1, STEP, ATT_W), _BF16),
                 jax.ShapeDtypeStruct((1, SUB, VT_W, TQ), _BF16)]
    out_specs = [_main_step_spec(MIX_HALF), _main_tiles_spec(ATT_W), _main_step_spec(ATT_W),
                 _main_tiles_spec(VT_W),
                 pl.BlockSpec((1, STEP, MIX_HALF), lambda s: (0, 0, 0)),
                 pl.BlockSpec((1, SUB, ATT_W, TQ), lambda s: (0, 0, 0, 0)),
                 pl.BlockSpec((1, STEP, ATT_W), lambda s: (0, 0, 0)),
                 pl.BlockSpec((1, SUB, VT_W, TQ), lambda s: (0, 0, 0, 0))]
    scratch = [pltpu.VMEM((TQ + 8, CONV_W), _F32), pltpu.VMEM((RET_W, RET_W), _F32),
               pltpu.VMEM((8, CONV_W), _F32), pltpu.VMEM((RET_W, RET_W), _F32)]
    return pl.pallas_call(
        _pre_kernel, grid=(1 + batch * NRS,), in_specs=in_specs, out_specs=out_specs, out_shape=out_shape,
        scratch_shapes=scratch, compiler_params=_params(), name="pre")(*args)


def _attn_call(batch, bounded, q, k, vt, k_m, vt_m):
    return pl.pallas_call(
        _attn_kernel, grid=(batch,),
        in_specs=[pl.BlockSpec(memory_space=pltpu.SMEM),
                  pl.BlockSpec((1, NRT, ATT_W, TQ), lambda b: (b, 0, 0, 0)),
                  pl.BlockSpec((1, SEQ, ATT_W), lambda b: (b, 0, 0)),
                  pl.BlockSpec((1, NRT, VT_W, TQ), lambda b: (b, 0, 0, 0)),
                  pl.BlockSpec((1, TQ, ATT_W), lambda b: (0, META_TILE, 0)),
                  pl.BlockSpec((1, 1, VT_W, TQ), lambda b: (0, META_TILE, 0, 0))],
        out_specs=pl.BlockSpec((1, NRT, O_W, TQ), lambda b: (b, 0, 0, 0)),
        out_shape=jax.ShapeDtypeStruct((batch, NRT, O_W, TQ), _BF16),
        scratch_shapes=[pltpu.VMEM((MLA_HEADS, VT_ROWS, TQ), _F32), pltpu.VMEM((MLA_HEADS, 1, TQ), _F32),
                        pltpu.VMEM((MLA_HEADS, TQ, TQ), _F32)],
        compiler_params=_params(), name="attn")(bounded, q, k, vt, k_m, vt_m)


def _attn_meta_call(q_m, k_m, vt_m):
    return pl.pallas_call(
        _attn_meta_kernel, grid=(1,),
        in_specs=[pl.BlockSpec((1, 1, ATT_W, TQ), lambda s: (0, META_TILE, 0, 0)),
                  pl.BlockSpec((1, TQ, ATT_W), lambda s: (0, META_TILE, 0)),
                  pl.BlockSpec((1, 1, VT_W, TQ), lambda s: (0, META_TILE, 0, 0))],
        out_specs=pl.BlockSpec((1, SUB, O_W, TQ), lambda s: (0, 0, 0, 0)),
        out_shape=jax.ShapeDtypeStruct((1, SUB, O_W, TQ), _BF16),
        compiler_params=_params(), name="attn_meta")(q_m, k_m, vt_m)


def _post_call(batch, x, x_meta, ycr, ycr_m, ot, ot_m, w):
    with_meta = x_meta is not None
    consts = [w[n] for n in ("w_out", "g_mlp", "w1", "w2")]
    const_specs = [_const_spec(c.shape) for c in consts]
    if with_meta:
        in_specs = ([_main_step_spec(D_MODEL), _const_spec((STEP, D_MODEL)),
                     _main_step_spec(MIX_HALF), pl.BlockSpec((1, STEP, MIX_HALF), lambda s: (0, 0, 0)),
                     _main_tiles_spec(O_W), pl.BlockSpec((1, SUB, O_W, TQ), lambda s: (0, 0, 0, 0))]
                    + const_specs)
        args = [x, x_meta, ycr, ycr_m, ot, ot_m] + consts
        out_specs = [_main_step_spec(D_MODEL), pl.BlockSpec((1, STEP, D_MODEL), lambda s: (0, 0, 0))]
        out_shape = [jax.ShapeDtypeStruct((batch, SEQ, D_MODEL), _F32),
                     jax.ShapeDtypeStruct((1, STEP, D_MODEL), _F32)]
        grid = (1 + batch * NRS,)
    else:
        in_specs = ([pl.BlockSpec((1, STEP, D_MODEL), lambda s: (s // NRS, s % NRS, 0)),
                     pl.BlockSpec((1, STEP, MIX_HALF), lambda s: (s // NRS, s % NRS, 0)),
                     pl.BlockSpec((1, SUB, O_W, TQ), lambda s: (s // NRS, s % NRS, 0, 0))] + const_specs)
        args = [x, ycr, ot] + consts
        out_specs = pl.BlockSpec((1, STEP, D_MODEL), lambda s: (s // NRS, s % NRS, 0))
        out_shape = jax.ShapeDtypeStruct((batch, SEQ, D_MODEL), _F32)
        grid = (batch * NRS,)
    return pl.pallas_call(
        functools.partial(_post_kernel, with_meta), grid=grid, in_specs=in_specs, out_specs=out_specs,
        out_shape=out_shape, compiler_params=_params(), name="post")(*args)


def kernel(x, meta_tokens, attn_norm_g, w_in, conv_w, ret_gn_g, q_norm_g, w_uq, kv_norm_g, w_ukv,
           q_head_norm_g, k_head_norm_g, w_out, mlp_norm_g, w_mlp_in, w_mlp_out):
    batch = x.shape[0]
    depth = w_in.shape[0]
    assert x.shape[1:] == (SEQ, D_MODEL) and depth >= 1
    t = _tables()
    h = x
    h_meta = jnp.concatenate([jnp.zeros((PAD, D_MODEL), x.dtype), meta_tokens.astype(x.dtype)], axis=0)
    for l in range(depth):
        w = _layer_weights(l, t, attn_norm_g, w_in, conv_w, ret_gn_g, q_norm_g, w_uq, kv_norm_g, w_ukv,
                           q_head_norm_g, k_head_norm_g, w_out, mlp_norm_g, w_mlp_in, w_mlp_out)
        ycr, q, k, vt, ycr_m, q_m, k_m, vt_m = _pre_call(batch, h, h_meta, w, t)
        ot = _attn_call(batch, w["bounded"], q, k, vt, k_m, vt_m)
        if l == depth - 1:
            h = _post_call(batch, h, None, ycr, None, ot, None, w)
        else:
            ot_m = _attn_meta_call(q_m, k_m, vt_m)
            h, h_meta3 = _post_call(batch, h, h_meta, ycr, ycr_m, ot, ot_m, w)
            h_meta = h_meta3[0]
    return h
```

```python
import functools
import math

import jax
import jax.numpy as jnp
from jax import lax
from jax.experimental import pallas as pl
from jax.experimental.pallas import tpu as pltpu

D_MODEL = 1024
SEQ = 2048
N_META = 16
CONV_W = 256
CONV_K = 3
RET_HEADS = 4
RET_DK = 64
RET_W = RET_HEADS * RET_DK
MLA_HEADS = 8
MLA_NOPE = 64
MLA_ROPE = 32
MLA_V = 64
MLA_QK = MLA_NOPE + MLA_ROPE
Q_LORA = 256
KV_LORA = 128
D_FF = 4 * D_MODEL
ROPE_BASE = 10000.0
NORM_EPS = 1e-6

LANES = 128
BF16_ROWS = 16
HEAD_W = LANES
ATT_W = MLA_HEADS * HEAD_W
VT_ROWS = MLA_V + BF16_ROWS
VT_W = MLA_HEADS * VT_ROWS
MIX_HALF = CONV_W + RET_W
O_W = MLA_HEADS * MLA_V

TQ = 256
SUB = 2
STEP = SUB * TQ
PAD = STEP - N_META
META_TILE = PAD // TQ
NRS = SEQ // STEP
NRT = SEQ // TQ

OFF_CB, OFF_CC, OFF_CH = 0, 256, 512
OFF_RQ, OFF_RK, OFF_RV, OFF_RG = 768, 1024, 1280, 1536
OFF_CQ, OFF_CKV, OFF_KR = 1792, 2048, 2176
D_IN_EXT = OFF_KR + HEAD_W

KV_UNROLL = 4
MASK_VALUE = -1e30
EXP2_SCALE = math.log2(math.e) / math.sqrt(MLA_QK)
MAX_SAFE_BOUND = 50.0
VMEM_LIMIT = 56 * 1024 * 1024

_F32 = jnp.float32
_BF16 = jnp.bfloat16
_NT_DIMS = (((1,), (1,)), ((), ()))
_TN_DIMS = (((0,), (0,)), ((), ()))


def _rms(x, n):
    return x * lax.rsqrt(jnp.sum(x * x, axis=-1, keepdims=True) * (1.0 / n) + NORM_EPS)


def _rot_pairs(z, half, first):
    w = z.shape[-1]
    return jnp.where(first, pltpu.roll(z, w - half, axis=1), pltpu.roll(z, half, axis=1))


def _meta_rows(xm):
    row = lax.broadcasted_iota(jnp.int32, (STEP, 1), 0)
    return jnp.where(row >= PAD, xm, 0.0)


def _pre_kernel(x_ref, xm_ref, *refs):
    consts = refs[:len(_PRE_CONSTS)]
    (ycr_ref, q_ref, k_ref, vt_ref, ycr_m, q_m, k_m, vt_m,
     ubuf, state, halo0, state0) = refs[len(_PRE_CONSTS):]
    s = pl.program_id(0)

    @pl.when(s == 0)
    def _():
        ubuf[0:8, :] = jnp.zeros((8, CONV_W), _F32)
        state[...] = jnp.zeros_like(state)
        x = _meta_rows(xm_ref[...])
        for sub in range(SUB):
            _pre_tile(True, sub, x[sub * TQ:(sub + 1) * TQ], consts, ycr_m, q_m, k_m, vt_m, ubuf, state)
        halo0[...] = ubuf[0:8, :]
        state0[...] = state[...]

    @pl.when(s > 0)
    def _():
        @pl.when((s - 1) % NRS == 0)
        def _():
            ubuf[0:8, :] = halo0[...]
            state[...] = state0[...]

        x = x_ref[0]
        for sub in range(SUB):
            _pre_tile(False, sub, x[sub * TQ:(sub + 1) * TQ], consts, ycr_ref, q_ref, k_ref, vt_ref, ubuf, state)


def _pre_tile(meta, sub, x, consts, ycr_ref, q_ref, k_ref, vt_ref, ubuf, state):
    (g_attn, w_in, conv_w, cosr, sinr, dmat, xi, zeta, cdec, bdmask, gmat, ret_g,
     qn_g, w_uq_t, w_uq_rot_t, kvn_g, w_k, w_vt, gpair, q_cos_t, q_sin_t, gk, cosm, sinm, qk_bound) = consts
    rows = slice(sub * TQ, (sub + 1) * TQ)
    hn = (_rms(x, D_MODEL) * g_attn[...]).astype(_BF16)
    proj_m = jnp.dot(hn, w_in[:, OFF_CQ:D_IN_EXT], preferred_element_type=_F32)
    proj_r = jnp.dot(hn, w_in[:, OFF_RQ:OFF_CQ], preferred_element_type=_F32)

    cq = proj_m[:, 0:Q_LORA]
    ckv = proj_m[:, Q_LORA:Q_LORA + KV_LORA]
    kr128 = proj_m[:, Q_LORA + KV_LORA:]
    cqn = (_rms(cq, Q_LORA) * qn_g[...]).astype(_BF16)
    ckvn = (_rms(ckv, KV_LORA) * kvn_g[...]).astype(_BF16)
    qraw = lax.dot_general(w_uq_t[...], cqn, _NT_DIMS, preferred_element_type=_F32)
    qrot = lax.dot_general(w_uq_rot_t[...], cqn, _NT_DIMS, preferred_element_type=_F32)
    kraw = jnp.dot(ckvn, w_k[...], preferred_element_type=_F32)
    vt = lax.dot_general(w_vt[...], ckvn, _NT_DIMS, preferred_element_type=_F32)
    vrow = lax.broadcasted_iota(jnp.int32, (VT_W, 1), 0)
    vt_ref[0, sub] = jnp.where((vrow % VT_ROWS) >= MLA_V, 1.0, vt).astype(_BF16)

    lane = lax.broadcasted_iota(jnp.int32, (1, RET_W), 1)
    first = (lane % RET_DK) < (RET_DK // 2)
    rq = proj_r[:, 0:RET_W]
    rk = proj_r[:, RET_W:2 * RET_W]
    rv = proj_r[:, 2 * RET_W:3 * RET_W]
    rg = proj_r[:, 3 * RET_W:]
    cr, sr = cosr[rows, :], sinr[rows, :]
    qr = rq * cr + _rot_pairs(rq, RET_DK // 2, first) * sr
    kr = (rk * cr + _rot_pairs(rk, RET_DK // 2, first) * sr) * (RET_DK ** -0.5)
    kb = kr.astype(_BF16)
    hmasks = [(lane // RET_DK) == h for h in range(RET_HEADS)]
    scs = [lax.dot_general(jnp.where(hmasks[h], qr, 0.0).astype(_BF16), kb, _NT_DIMS,
                           preferred_element_type=_F32) for h in range(RET_HEADS)]

    g2 = gpair[...]
    q2 = (qraw * qraw).astype(_BF16)
    k2 = (kraw * kraw).astype(_BF16)
    kr_ms = jnp.dot((kr128 * kr128).astype(_BF16), g2[0:HEAD_W, 0:HEAD_W], preferred_element_type=_F32)
    kr_ms2 = jnp.concatenate([kr_ms, kr_ms], axis=1)
    rq2, rk2 = [], []
    for pr in range(MLA_HEADS // 2):
        psl = slice(2 * pr * HEAD_W, (2 * pr + 2) * HEAD_W)
        q_sq = jnp.dot(g2, q2[psl, :], preferred_element_type=_F32)
        k_sq = jnp.dot(k2[:, psl], g2, preferred_element_type=_F32) + kr_ms2
        rq2.append(lax.rsqrt(q_sq * (1.0 / MLA_QK) + NORM_EPS))
        rk2.append(lax.rsqrt(k_sq * (1.0 / MLA_QK) + NORM_EPS))

    proj_c = jnp.dot(hn, w_in[:, OFF_CB:OFF_RQ], preferred_element_type=_F32)

    st = state[...]
    cross = jnp.dot(qr.astype(_BF16), st.astype(_BF16), preferred_element_type=_F32) * xi[...]
    upd = lax.dot_general((kr * zeta[...]).astype(_BF16), rv.astype(_BF16), _TN_DIMS,
                          preferred_element_type=_F32)
    state[...] = cdec[...] * st + bdmask[...] * upd
    inner = jnp.zeros((TQ, RET_W), _F32)
    for h in range(RET_HEADS):
        sc = (scs[h] * dmat[h]).astype(_BF16)
        vh = jnp.where(hmasks[h], rv, 0.0).astype(_BF16)
        inner = inner + jnp.dot(sc, vh, preferred_element_type=_F32)
    o = inner + cross
    msq = jnp.dot((o * o).astype(_BF16), gmat[...], preferred_element_type=_F32)

    lane_h = lax.broadcasted_iota(jnp.int32, (1, HEAD_W), 1)
    first_m = lane_h < (MLA_NOPE + MLA_ROPE // 2)
    cm, sm = cosm[rows, :], sinm[rows, :]
    gkv = gk[...]
    q_cos, q_sin = q_cos_t[:, rows], q_sin_t[:, rows]
    krg = kr128 * gkv
    k_rope = krg * cm + _rot_pairs(krg, MLA_ROPE // 2, first_m) * sm
    one_q = jnp.where(lax.broadcasted_iota(jnp.int32, (HEAD_W, 1), 0) == MLA_QK, 1.0, 0.0)
    kbias = jnp.where(lane_h == MLA_QK, -qk_bound[0], 0.0)
    if meta:
        rowm = sub * TQ + lax.broadcasted_iota(jnp.int32, (TQ, 1), 0)
        kbias = jnp.where((rowm < PAD) & (lane_h == MLA_QK), MASK_VALUE, kbias)
    for h in range(MLA_HEADS):
        sl = slice(h * HEAD_W, (h + 1) * HEAD_W)
        rsl = slice((h % 2) * HEAD_W, (h % 2 + 1) * HEAD_W)
        q_ref[0, sub, sl, :] = (rq2[h // 2][rsl, :] * (qraw[sl, :] * q_cos + qrot[sl, :] * q_sin)
                                + one_q).astype(_BF16)
        k_ref[0, rows, sl] = (rk2[h // 2][:, rsl] * (kraw[:, sl] * gkv + k_rope) + kbias).astype(_BF16)

    cb = proj_c[:, OFF_CB:OFF_CB + CONV_W]
    u = proj_c[:, OFF_CC:OFF_CC + CONV_W] * proj_c[:, OFF_CH:OFF_CH + CONV_W]
    ubuf[8:8 + TQ, :] = u
    cw = conv_w[...]
    conv = cw[0:1] * ubuf[6:6 + TQ, :] + cw[1:2] * ubuf[7:7 + TQ, :] + cw[2:3] * u
    ubuf[0:8, :] = u[TQ - 8:, :]
    ycr_ref[0, rows, 0:CONV_W] = (cb * conv).astype(_BF16)

    o_n = o * lax.rsqrt(msq + NORM_EPS) * ret_g[...]
    gate = rg * (1.0 / (1.0 + jnp.exp(-rg)))
    ycr_ref[0, rows, CONV_W:MIX_HALF] = (gate * o_n).astype(_BF16)


def _attn_kernel(bounded_ref, q_ref, k_ref, vt_ref, km_ref, vtm_ref, ot_ref, acc_scr, m_scr, s_scr):
    bounded = bounded_ref[0] != 0
    run_all = functools.partial(_attn_pipeline, q_ref, k_ref, vt_ref, km_ref, vtm_ref, ot_ref, acc_scr, m_scr,
                                s_scr)
    pl.when(bounded)(functools.partial(run_all, True))
    pl.when(jnp.logical_not(bounded))(functools.partial(run_all, False))


_META_BLOCK = "meta"


def _attn_pipeline(q_ref, k_ref, vt_ref, km_ref, vtm_ref, ot_ref, acc_scr, m_scr, s_scr, bounded):
    causal = (lax.broadcasted_iota(jnp.int32, (TQ, 1), 0) <= lax.broadcasted_iota(jnp.int32, (1, TQ), 1))
    heads = [slice(h * HEAD_W, (h + 1) * HEAD_W) for h in range(MLA_HEADS)]
    vheads = [slice(h * VT_ROWS, (h + 1) * VT_ROWS) for h in range(MLA_HEADS)]

    def tile_start(t):
        return t * TQ if isinstance(t, int) else pl.multiple_of(t * TQ, TQ)

    def scores(i, j, h):
        if j is _META_BLOCK:
            keys = km_ref[0, :, heads[h]]
        else:
            keys = k_ref[0, pl.ds(tile_start(j), TQ), heads[h]]
        return jnp.dot(keys, q_ref[0, i, heads[h], :], preferred_element_type=_F32)

    def stage(j, nxt, diagonal=False):
        for h in range(MLA_HEADS):
            s = s_scr[h]
            s_scr[h] = scores(nxt[0], nxt[1], h)
            if diagonal:
                s = jnp.where(causal, s, MASK_VALUE)
            vth = vtm_ref[0, 0, vheads[h], :] if j is _META_BLOCK else vt_ref[0, j, vheads[h], :]
            if bounded:
                acc_scr[h] = acc_scr[h] + jnp.dot(vth, jnp.exp2(s).astype(_BF16), preferred_element_type=_F32)
                continue
            m_prev = m_scr[h]
            m_new = jnp.maximum(m_prev, jnp.max(s, axis=0, keepdims=True))
            p = jnp.exp2(s - m_new)
            alpha = jnp.exp2(m_prev - m_new)
            pv = jnp.dot(vth, p.astype(_BF16), preferred_element_type=_F32)
            acc_scr[h] = acc_scr[h] * alpha + pv
            m_scr[h] = m_new

    for h in range(MLA_HEADS):
        s_scr[h] = scores(0, 0, h)

    def tile_body(i, carry):
        if not bounded:
            m_scr[...] = jnp.full(m_scr.shape, MASK_VALUE, _F32)
        acc_scr[...] = jnp.zeros(acc_scr.shape, _F32)

        def run(j0, n):
            for u in range(n):
                stage(j0 + u, (i, j0 + u + 1))

        def unrolled_body(jj, c):
            run(jj * KV_UNROLL, KV_UNROLL)
            return c

        def finish(rem):
            run(i - rem, rem)
            stage(i, (i, _META_BLOCK), diagonal=True)
            nxt_tile = jnp.minimum(i + 1, NRT - 1)
            stage(_META_BLOCK, (nxt_tile, 0))
            for h in range(MLA_HEADS):
                a = acc_scr[h]
                inv = 1.0 / a[MLA_V:MLA_V + 8]
                ot_ref[0, i, h * MLA_V:(h + 1) * MLA_V, :] = (
                    a[0:MLA_V] * jnp.concatenate([inv] * (MLA_V // 8), axis=0)).astype(_BF16)

        lax.fori_loop(0, i // KV_UNROLL, unrolled_body, 0)
        for rem in range(KV_UNROLL):
            pl.when(i % KV_UNROLL == rem)(functools.partial(finish, rem))
        return carry

    lax.fori_loop(0, NRT, tile_body, 0)


def _attn_meta_kernel(q_ref, k_ref, vt_ref, ot_ref):
    causal = (lax.broadcasted_iota(jnp.int32, (TQ, 1), 0) <= lax.broadcasted_iota(jnp.int32, (1, TQ), 1))
    for t in range(SUB):
        if t != META_TILE:
            ot_ref[0, t] = jnp.zeros((O_W, TQ), _BF16)
    for h in range(MLA_HEADS):
        sl = slice(h * HEAD_W, (h + 1) * HEAD_W)
        s = jnp.dot(k_ref[0, :, sl], q_ref[0, 0, sl, :], preferred_element_type=_F32)
        s = jnp.where(causal, s, MASK_VALUE)
        p = jnp.exp2(s - jnp.max(s, axis=0, keepdims=True))
        a = jnp.dot(vt_ref[0, 0, h * VT_ROWS:(h + 1) * VT_ROWS, :], p.astype(_BF16), preferred_element_type=_F32)
        inv = 1.0 / a[MLA_V:MLA_V + 8]
        ot_ref[0, META_TILE, h * MLA_V:(h + 1) * MLA_V, :] = (
            a[0:MLA_V] * jnp.concatenate([inv] * (MLA_V // 8), axis=0)).astype(_BF16)


def _post_step(x, ycr_ref, ot_ref, w_out, g_mlp, w1, w2, out_ref):
    tiles = [slice(t * TQ, (t + 1) * TQ) for t in range(SUB)]
    x1 = [x[r] + jnp.dot(ycr_ref[0, r, :], w_out[0:MIX_HALF, :], preferred_element_type=_F32)
          + lax.dot_general(ot_ref[0, t], w_out[MIX_HALF:, :], _TN_DIMS, preferred_element_type=_F32)
          for t, r in enumerate(tiles)]
    up = [jnp.dot((_rms(v, D_MODEL) * g_mlp[...]).astype(_BF16), w1[...], preferred_element_type=_F32)
          for v in x1]
    down = [jnp.dot(jnp.square(jnp.maximum(u, 0.0)).astype(_BF16), w2[...], preferred_element_type=_F32)
            for u in up]
    for t, r in enumerate(tiles):
        out_ref[0, r, :] = x1[t] + down[t]


def _post_kernel(with_meta, *refs):
    if not with_meta:
        x_ref, ycr_ref, ot_ref, w_out, g_mlp, w1, w2, out_ref = refs
        _post_step(x_ref[0], ycr_ref, ot_ref, w_out, g_mlp, w1, w2, out_ref)
        return
    x_ref, xm_ref, ycr_ref, ycr_m, ot_ref, ot_m, w_out, g_mlp, w1, w2, out_ref, out_m = refs
    s = pl.program_id(0)

    @pl.when(s == 0)
    def _():
        _post_step(_meta_rows(xm_ref[...]), ycr_m, ot_m, w_out, g_mlp, w1, w2, out_m)

    @pl.when(s > 0)
    def _():
        _post_step(x_ref[0], ycr_ref, ot_ref, w_out, g_mlp, w1, w2, out_ref)


def _const_spec(shape):
    nd = len(shape)
    return pl.BlockSpec(shape, lambda s, _nd=nd: (0,) * _nd, pipeline_mode=pl.Buffered(1))


def _params():
    return pltpu.CompilerParams(dimension_semantics=("arbitrary",), vmem_limit_bytes=VMEM_LIMIT)


def _batch_step(s):
    r = jnp.maximum(s - 1, 0)
    return r // NRS, r % NRS


def _rope_partner(lh):
    half = MLA_ROPE // 2
    return jnp.where((lh >= MLA_NOPE) & (lh < MLA_NOPE + half), lh + half,
                     jnp.where((lh >= MLA_NOPE + half) & (lh < MLA_QK), lh - half, lh))


def _tables():
    pos = jnp.maximum(jnp.arange(STEP + SEQ, dtype=_F32) - PAD, 0.0)
    inv_r = 1.0 / (ROPE_BASE ** (jnp.arange(0, RET_DK, 2, dtype=_F32) / RET_DK))
    ang_r = pos[:, None] * inv_r[None, :]
    l = jnp.arange(RET_W)
    cosr = jnp.cos(ang_r)[:, l % (RET_DK // 2)]
    sinr = jnp.sin(ang_r)[:, l % (RET_DK // 2)] * jnp.where((l % RET_DK) < RET_DK // 2, -1.0, 1.0)[None, :]
    inv_m = 1.0 / (ROPE_BASE ** (jnp.arange(0, MLA_ROPE, 2, dtype=_F32) / MLA_ROPE))
    ang_m = pos[:, None] * inv_m[None, :]
    lh = jnp.arange(HEAD_W)
    in_rope = (lh >= MLA_NOPE) & (lh < MLA_QK)
    fidx = jnp.clip(lh - MLA_NOPE, 0, MLA_ROPE - 1) % (MLA_ROPE // 2)
    cosm = jnp.where(in_rope[None, :], jnp.cos(ang_m)[:, fidx], 1.0)
    sgn = jnp.where(lh < MLA_NOPE + MLA_ROPE // 2, -1.0, 1.0)
    sinm = jnp.where(in_rope[None, :], jnp.sin(ang_m)[:, fidx] * sgn[None, :], 0.0)
    log_g = jnp.log1p(-jnp.exp2(-5.0 - jnp.arange(RET_HEADS, dtype=_F32)))
    idx = jnp.arange(TQ, dtype=_F32)
    diff = idx[:, None] - idx[None, :]
    dmat = jnp.where(diff >= 0, jnp.exp(jnp.maximum(diff, 0.0)[None] * log_g[:, None, None]), 0.0)
    lg_lane = log_g[l // RET_DK]
    xi = jnp.exp((idx[:, None] + 1.0) * lg_lane[None, :])
    zeta = jnp.exp((TQ - 1.0 - idx[:, None]) * lg_lane[None, :])
    cdec = jnp.exp(TQ * lg_lane)[None, :]
    same = (l[:, None] // RET_DK) == (l[None, :] // RET_DK)
    bdmask = same.astype(_F32)
    gmat = (same.astype(_F32) / RET_DK).astype(_BF16)
    l2 = jnp.arange(2 * HEAD_W)
    gpair = ((l2[:, None] // HEAD_W) == (l2[None, :] // HEAD_W)).astype(_BF16)
    return dict(cosr=cosr, sinr=sinr, cosm=cosm, sinm=sinm, dmat=dmat, xi=xi, zeta=zeta, cdec=cdec,
                bdmask=bdmask, gmat=gmat, gpair=gpair)


def _layer_weights(l, t, attn_norm_g, w_in, conv_w, ret_gn_g, q_norm_g, w_uq, kv_norm_g, w_ukv,
                   q_head_norm_g, k_head_norm_g, w_out, mlp_norm_g, w_mlp_in, w_mlp_out):
    wi = w_in[l]
    w_in_ext = jnp.concatenate(
        [wi[:, :OFF_KR], jnp.zeros((D_MODEL, MLA_NOPE), _F32), wi[:, OFF_KR:],
         jnp.zeros((D_MODEL, HEAD_W - MLA_QK), _F32)], axis=1).astype(_BF16)
    partner = _rope_partner(jnp.arange(HEAD_W))
    is_rope = partner != jnp.arange(HEAD_W)
    wq = jnp.pad(w_uq[l].reshape(Q_LORA, MLA_HEADS, MLA_QK), ((0, 0), (0, 0), (0, HEAD_W - MLA_QK)))
    wq_rot = jnp.where(is_rope[None, None, :], wq[:, :, partner], 0.0)
    wkv = w_ukv[l].reshape(KV_LORA, MLA_HEADS, MLA_NOPE + MLA_V)
    w_k = jnp.pad(wkv[:, :, :MLA_NOPE], ((0, 0), (0, 0), (0, HEAD_W - MLA_NOPE))).reshape(KV_LORA, ATT_W)
    w_vt = jnp.pad(wkv[:, :, MLA_NOPE:], ((0, 0), (0, 0), (0, VT_ROWS - MLA_V))).reshape(KV_LORA, VT_W).T
    gq128 = jnp.pad(q_head_norm_g[l], (0, HEAD_W - MLA_QK)) * EXP2_SCALE
    gk128 = jnp.pad(k_head_norm_g[l], (0, HEAD_W - MLA_QK))
    qk_bound = 1.01 * MLA_QK * jnp.max(jnp.abs(gq128)) * jnp.max(jnp.abs(gk128))
    return dict(
        qk_bound=qk_bound.reshape(1).astype(_F32),
        bounded=(qk_bound <= MAX_SAFE_BOUND).reshape(1).astype(jnp.int32),
        g_attn=attn_norm_g[l][None, :], w_in=w_in_ext, conv_w=conv_w[l], ret_g=ret_gn_g[l].reshape(1, RET_W),
        qn_g=q_norm_g[l][None, :], w_uq_t=wq.reshape(Q_LORA, ATT_W).T.astype(_BF16),
        w_uq_rot_t=wq_rot.reshape(Q_LORA, ATT_W).T.astype(_BF16), kvn_g=kv_norm_g[l][None, :],
        w_k=w_k.astype(_BF16), w_vt=w_vt.astype(_BF16),
        q_cos_t=gq128[:, None] * t["cosm"].T,
        q_sin_t=jnp.where(is_rope, gq128[partner], 0.0)[:, None] * t["sinm"].T, gk=gk128[None, :],
        w_out=w_out[l].astype(_BF16), g_mlp=mlp_norm_g[l][None, :], w1=w_mlp_in[l].astype(_BF16),
        w2=w_mlp_out[l].astype(_BF16))


_PRE_CONSTS = [("g_attn", "w"), ("w_in", "w"), ("conv_w", "w"), ("cosr", "row"), ("sinr", "row"), ("dmat", "t"),
               ("xi", "t"), ("zeta", "t"), ("cdec", "t"), ("bdmask", "t"), ("gmat", "t"), ("ret_g", "w"),
               ("qn_g", "w"), ("w_uq_t", "w"), ("w_uq_rot_t", "w"), ("kvn_g", "w"), ("w_k", "w"), ("w_vt", "w"),
               ("gpair", "t"), ("q_cos_t", "col"), ("q_sin_t", "col"), ("gk", "w"), ("cosm", "row"),
               ("sinm", "row"), ("qk_bound", "smem")]


def _main_step_spec(width):
    def index(s):
        b, i = _batch_step(s)
        return b, i, 0
    return pl.BlockSpec((1, STEP, width), index)


def _main_tiles_spec(rows):
    def index(s):
        b, i = _batch_step(s)
        return b, i, 0, 0
    return pl.BlockSpec((1, SUB, rows, TQ), index)


def _pre_call(batch, x, x_meta, w, t):
    in_specs = [_main_step_spec(D_MODEL), _const_spec((STEP, D_MODEL))]
    args = [x, x_meta]
    for name, kind in _PRE_CONSTS:
        arr = t[name] if kind in ("t", "row") else w[name]
        if kind == "row":
            in_specs.append(pl.BlockSpec((STEP, arr.shape[1]),
                                         lambda s: (jnp.where(s == 0, 0, _batch_step(s)[1] + 1), 0)))
        elif kind == "col":
            in_specs.append(pl.BlockSpec((arr.shape[0], STEP),
                                         lambda s: (0, jnp.where(s == 0, 0, _batch_step(s)[1] + 1))))
        elif kind == "smem":
            in_specs.append(pl.BlockSpec(memory_space=pltpu.SMEM))
        else:
            in_specs.append(_const_spec(arr.shape))
        args.append(arr)
    out_shape = [jax.ShapeDtypeStruct((batch, SEQ, MIX_HALF), _BF16),
                 jax.ShapeDtypeStruct((batch, NRT, ATT_W, TQ), _BF16),
                 jax.ShapeDtypeStruct((batch, SEQ, ATT_W), _BF16),
                 jax.ShapeDtypeStruct((batch, NRT, VT_W, TQ), _BF16),
                 jax.ShapeDtypeStruct((1, STEP, MIX_HALF), _BF16),
                 jax.ShapeDtypeStruct((1, SUB, ATT_W, TQ), _BF16),
                 jax.ShapeDtypeStruct((1, STEP, ATT_W), _BF16),
                 jax.ShapeDtypeStruct((1, SUB, VT_W, TQ), _BF16)]
    out_specs = [_main_step_spec(MIX_HALF), _main_tiles_spec(ATT_W), _main_step_spec(ATT_W),
                 _main_tiles_spec(VT_W),
                 pl.BlockSpec((1, STEP, MIX_HALF), lambda s: (0, 0, 0)),
                 pl.BlockSpec((1, SUB, ATT_W, TQ), lambda s: (0, 0, 0, 0)),
                 pl.BlockSpec((1, STEP, ATT_W), lambda s: (0, 0, 0)),
                 pl.BlockSpec((1, SUB, VT_W, TQ), lambda s: (0, 0, 0, 0))]
    scratch = [pltpu.VMEM((TQ + 8, CONV_W), _F32), pltpu.VMEM((RET_W, RET_W), _F32),
               pltpu.VMEM((8, CONV_W), _F32), pltpu.VMEM((RET_W, RET_W), _F32)]
    return pl.pallas_call(
        _pre_kernel, grid=(1 + batch * NRS,), in_specs=in_specs, out_specs=out_specs, out_shape=out_shape,
        scratch_shapes=scratch, compiler_params=_params(), name="pre")(*args)


def _attn_call(batch, bounded, q, k, vt, k_m, vt_m):
    return pl.pallas_call(
        _attn_kernel, grid=(batch,),
        in_specs=[pl.BlockSpec(memory_space=pltpu.SMEM),
                  pl.BlockSpec((1, NRT, ATT_W, TQ), lambda b: (b, 0, 0, 0)),
                  pl.BlockSpec((1, SEQ, ATT_W), lambda b: (b, 0, 0)),
                  pl.BlockSpec((1, NRT, VT_W, TQ), lambda b: (b, 0, 0, 0)),
                  pl.BlockSpec((1, TQ, ATT_W), lambda b: (0, META_TILE, 0)),
                  pl.BlockSpec((1, 1, VT_W, TQ), lambda b: (0, META_TILE, 0, 0))],
        out_specs=pl.BlockSpec((1, NRT, O_W, TQ), lambda b: (b, 0, 0, 0)),
        out_shape=jax.ShapeDtypeStruct((batch, NRT, O_W, TQ), _BF16),
        scratch_shapes=[pltpu.VMEM((MLA_HEADS, VT_ROWS, TQ), _F32), pltpu.VMEM((MLA_HEADS, 1, TQ), _F32),
                        pltpu.VMEM((MLA_HEADS, TQ, TQ), _F32)],
        compiler_params=_params(), name="attn")(bounded, q, k, vt, k_m, vt_m)


def _attn_meta_call(q_m, k_m, vt_m):
    return pl.pallas_call(
        _attn_meta_kernel, grid=(1,),
        in_specs=[pl.BlockSpec((1, 1, ATT_W, TQ), lambda s: (0, META_TILE, 0, 0)),
                  pl.BlockSpec((1, TQ, ATT_W), lambda s: (0, META_TILE, 0)),
                  pl.BlockSpec((1, 1, VT_W, TQ), lambda s: (0, META_TILE, 0, 0))],
        out_specs=pl.BlockSpec((1, SUB, O_W, TQ), lambda s: (0, 0, 0, 0)),
        out_shape=jax.ShapeDtypeStruct((1, SUB, O_W, TQ), _BF16),
        compiler_params=_params(), name="attn_meta")(q_m, k_m, vt_m)


def _post_call(batch, x, x_meta, ycr, ycr_m, ot, ot_m, w):
    with_meta = x_meta is not None
    consts = [w[n] for n in ("w_out", "g_mlp", "w1", "w2")]
    const_specs = [_const_spec(c.shape) for c in consts]
    if with_meta:
        in_specs = ([_main_step_spec(D_MODEL), _const_spec((STEP, D_MODEL)),
                     _main_step_spec(MIX_HALF), pl.BlockSpec((1, STEP, MIX_HALF), lambda s: (0, 0, 0)),
                     _main_tiles_spec(O_W), pl.BlockSpec((1, SUB, O_W, TQ), lambda s: (0, 0, 0, 0))]
                    + const_specs)
        args = [x, x_meta, ycr, ycr_m, ot, ot_m] + consts
        out_specs = [_main_step_spec(D_MODEL), pl.BlockSpec((1, STEP, D_MODEL), lambda s: (0, 0, 0))]
        out_shape = [jax.ShapeDtypeStruct((batch, SEQ, D_MODEL), _F32),
                     jax.ShapeDtypeStruct((1, STEP, D_MODEL), _F32)]
        grid = (1 + batch * NRS,)
    else:
        in_specs = ([pl.BlockSpec((1, STEP, D_MODEL), lambda s: (s // NRS, s % NRS, 0)),
                     pl.BlockSpec((1, STEP, MIX_HALF), lambda s: (s // NRS, s % NRS, 0)),
                     pl.BlockSpec((1, SUB, O_W, TQ), lambda s: (s // NRS, s % NRS, 0, 0))] + const_specs)
        args = [x, ycr, ot] + consts
        out_specs = pl.BlockSpec((1, STEP, D_MODEL), lambda s: (s // NRS, s % NRS, 0))
        out_shape = jax.ShapeDtypeStruct((batch, SEQ, D_MODEL), _F32)
        grid = (batch * NRS,)
    return pl.pallas_call(
        functools.partial(_post_kernel, with_meta), grid=grid, in_specs=in_specs, out_specs=out_specs,
        out_shape=out_shape, compiler_params=_params(), name="post")(*args)


def kernel(x, meta_tokens, attn_norm_g, w_in, conv_w, ret_gn_g, q_norm_g, w_uq, kv_norm_g, w_ukv,
           q_head_norm_g, k_head_norm_g, w_out, mlp_norm_g, w_mlp_in, w_mlp_out):
    batch = x.shape[0]
    depth = w_in.shape[0]
    assert x.shape[1:] == (SEQ, D_MODEL) and depth >= 1
    t = _tables()
    h = x
    h_meta = jnp.concatenate([jnp.zeros((PAD, D_MODEL), x.dtype), meta_tokens.astype(x.dtype)], axis=0)
    for l in range(depth):
        w = _layer_weights(l, t, attn_norm_g, w_in, conv_w, ret_gn_g, q_norm_g, w_uq, kv_norm_g, w_ukv,
                           q_head_norm_g, k_head_norm_g, w_out, mlp_norm_g, w_mlp_in, w_mlp_out)
        ycr, q, k, vt, ycr_m, q_m, k_m, vt_m = _pre_call(batch, h, h_meta, w, t)
        ot = _attn_call(batch, w["bounded"], q, k, vt, k_m, vt_m)
        if l == depth - 1:
            h = _post_call(batch, h, None, ycr, None, ot, None, w)
        else:
            ot_m = _attn_meta_call(q_m, k_m, vt_m)
            h, h_meta3 = _post_call(batch, h, h_meta, ycr, ycr_m, ot, ot_m, w)
            h_meta = h_meta3[0]
    return h
```
